```python
import math
import jax, jax.numpy as jnp
from jax import lax
import numpy as np

D_MODEL = 1024
BATCH = 16
SEQ = 2048
DEPTH = 2
DEC_BATCH = 128
DEC_SEQ = 4
PAST_LEN = 16384
PAGE_SIZE = 128

MLA_HEADS = 8
QK_NOPE = 64
ROPE_DIM = 32
V_DIM = 64
Q_RANK = 256
KV_RANK = 128
ROPE_THETA = 10000.0
MLA_SCALE = (QK_NOPE + ROPE_DIM) ** -0.5
SSM_WIDTH = 512
SSM_GROUP_SIZE = 16
SSM_GROUPS = SSM_WIDTH // SSM_GROUP_SIZE
SSM_STATE = 64
DT_MIN = 1e-3
DT_MAX = 1e-1
SB_HEADS = 8
SB_KV_HEADS = 2
SB_HEAD_DIM = 64
SB_GROUP = SB_HEADS // SB_KV_HEADS
SB_SCALE = SB_HEAD_DIM ** -0.5
D_FF = 2816
N_EXPERTS = 8
TOP_K = 2
D_EXPERT = 3584
N_DENSE = (DEPTH + 1) // 2
N_MOE = DEPTH // 2
Q_BLOCK = 128
EPS = 1e-6
IN_SIZES = (Q_RANK, KV_RANK, ROPE_DIM, SSM_WIDTH, SB_HEADS * SB_HEAD_DIM,
            SB_KV_HEADS * SB_HEAD_DIM, SB_KV_HEADS * SB_HEAD_DIM, D_MODEL, D_MODEL, D_MODEL)
IN_WIDTH = sum(IN_SIZES)

kernel_name = 'hybrid_mla_s5_stickbreak_decoder_step'


def rmsnorm(x, g):
    xf = x.astype(jnp.float32)
    y = xf * lax.rsqrt(jnp.mean(xf * xf, axis=-1, keepdims=True) + EPS)
    return (y * g.astype(jnp.float32)).astype(x.dtype)


def split_last(z, sizes):
    out, start = [], 0
    for s in sizes:
        out.append(z[..., start:start + s])
        start += s
    return out


def rope_cos_sin(pos):
    inv = 1.0 / (ROPE_THETA ** (jnp.arange(0, ROPE_DIM, 2, dtype=jnp.float32) / ROPE_DIM))
    ang = pos.astype(jnp.float32)[:, None] * inv[None, :]
    return jnp.cos(ang), jnp.sin(ang)


def apply_rope(x, cos, sin):
    half = ROPE_DIM // 2
    x1 = x[..., :half].astype(jnp.float32)
    x2 = x[..., half:].astype(jnp.float32)
    return jnp.concatenate([x1 * cos - x2 * sin, x1 * sin + x2 * cos], axis=-1).astype(x.dtype)


def causal_sweep(attend, qs, ks, q_start):
    tq = qs[0].shape[1]
    outs = []
    for start in range(0, tq, Q_BLOCK):
        stop = min(start + Q_BLOCK, tq)
        n_k = q_start + stop
        q_pos = q_start + jnp.arange(start, stop)
        k_pos = jnp.arange(n_k)
        outs.append(attend(*[a[:, start:stop] for a in qs], *[a[:, :n_k] for a in ks], q_pos, k_pos))
    return jnp.concatenate(outs, axis=1)


def mla_attend(q_lat, q_rope, lat, krope, q_pos, k_pos):
    s = (jnp.einsum('bthr,bsr->bhts', q_lat, lat)
         + jnp.einsum('bthd,bsd->bhts', q_rope, krope)).astype(jnp.float32) * MLA_SCALE
    s = jnp.where(k_pos[None, :] <= q_pos[:, None], s, -jnp.inf)
    p = jax.nn.softmax(s, axis=-1).astype(lat.dtype)
    return jnp.einsum('bhts,bsr->bthr', p, lat)


def sb_attend(q, k, v, q_pos, k_pos):
    bsz, tq = q.shape[0], q.shape[1]
    z = jnp.einsum('btkgd,bskd->bkgts', q, k).astype(jnp.float32) * SB_SCALE
    strict = k_pos[None, :] < q_pos[:, None]
    log_1m = jnp.where(strict, jax.nn.log_sigmoid(-z), 0.0)
    between = lax.cumsum(log_1m, axis=4, reverse=True) - log_1m
    a = jnp.where(strict, jnp.exp(jax.nn.log_sigmoid(z) + between), 0.0)
    o = jnp.einsum('bkgts,bskd->btkgd', a.astype(v.dtype), v)
    return o.reshape(bsz, tq, SB_HEADS * SB_HEAD_DIM)


def s5_branch(u, h0, lam_re, lam_im, log_dt, b_re, b_im, c_re, c_im, d_skip, w_glu, b_glu):
    bsz, t, _ = u.shape
    ug = u.reshape(bsz, t, SSM_GROUPS, SSM_GROUP_SIZE)
    dt = jnp.exp(log_dt)[:, None]
    decay = jnp.exp(lam_re * dt)
    ab_re, ab_im = decay * jnp.cos(lam_im * dt), decay * jnp.sin(lam_im * dt)
    inv_den = 1.0 / (lam_re * lam_re + lam_im * lam_im)
    nr, ni = ab_re - 1.0, ab_im
    co_re = (nr * lam_re + ni * lam_im) * inv_den
    co_im = (ni * lam_re - nr * lam_im) * inv_den
    bb_re = co_re[..., None] * b_re - co_im[..., None] * b_im
    bb_im = co_re[..., None] * b_im + co_im[..., None] * b_re
    bu_re = jnp.einsum('btgp,gnp->btgn', ug, bb_re)
    bu_im = jnp.einsum('btgp,gnp->btgn', ug, bb_im)
    a_re = jnp.broadcast_to(ab_re, (1, t) + ab_re.shape)
    a_im = jnp.broadcast_to(ab_im, (1, t) + ab_im.shape)

    def combine(e1, e2):
        ar1, ai1, br1, bi1 = e1
        ar2, ai2, br2, bi2 = e2
        return (ar2 * ar1 - ai2 * ai1, ar2 * ai1 + ai2 * ar1,
                ar2 * br1 - ai2 * bi1 + br2, ar2 * bi1 + ai2 * br1 + bi2)

    p_re, p_im, h_re, h_im = lax.associative_scan(combine, (a_re, a_im, bu_re, bu_im), axis=1)
    if h0 is not None:
        h0_re, h0_im = h0[0][:, None], h0[1][:, None]
        h_re, h_im = (h_re + p_re * h0_re - p_im * h0_im, h_im + p_re * h0_im + p_im * h0_re)
    y = jnp.einsum('btgn,gpn->btgp', h_re, c_re) - jnp.einsum('btgn,gpn->btgp', h_im, c_im)
    y = y.reshape(bsz, t, SSM_WIDTH) + d_skip * u
    g = jax.nn.gelu(y)
    return g * jax.nn.sigmoid(g @ w_glu + b_glu), h_re[:, -1], h_im[:, -1]


def swiglu(h, wg, wu, wd):
    return (jax.nn.silu(h @ wg) * (h @ wu)) @ wd


def moe_ffn(h, router, wg, wu, wd):
    logits = jnp.einsum('btd,de->bte', h, router).astype(jnp.float32)
    top_val, top_idx = lax.top_k(logits, TOP_K)
    probs = jax.nn.softmax(top_val, axis=-1)
    gate = jnp.einsum('btk,btke->bte', probs,
                      jax.nn.one_hot(top_idx, N_EXPERTS, dtype=jnp.float32)).astype(h.dtype)
    out = jnp.zeros_like(h)
    for e in range(N_EXPERTS):
        out = out + gate[..., e:e + 1] * swiglu(h, wg[e], wu[e], wd[e])
    return out


def gather_pages(pool, l, page_table):
    g = pool[l, page_table]
    return g.reshape((g.shape[0], g.shape[1] * g.shape[2]) + g.shape[3:])


def token_mixer(h, q_start, past, l, w):
    bsz, t, _ = h.shape
    (c_q, c_kv, k_rope, u, q_sb, k_sb, v_sb,
     gate_a, gate_b, gate_c) = split_last(h @ w['w_in'][l], IN_SIZES)
    cos, sin = rope_cos_sin(q_start + jnp.arange(t))
    q = (rmsnorm(c_q, w['g_q'][l]) @ w['w_uq'][l]).reshape(bsz, t, MLA_HEADS, QK_NOPE + ROPE_DIM)
    q_nope = q[..., :QK_NOPE]
    q_rope = apply_rope(q[..., QK_NOPE:], cos[:, None], sin[:, None])
    lat = rmsnorm(c_kv, w['g_kv'][l])
    k_rope = apply_rope(k_rope, cos, sin)
    q_lat = jnp.einsum('bthd,rhd->bthr', q_nope, w['w_uk'][l])
    q_sb = q_sb.reshape(bsz, t, SB_KV_HEADS, SB_GROUP, SB_HEAD_DIM)
    k_sb = k_sb.reshape(bsz, t, SB_KV_HEADS, SB_HEAD_DIM)
    v_sb = v_sb.reshape(bsz, t, SB_KV_HEADS, SB_HEAD_DIM)
    if past is None:
        keys_a, keys_c, h0 = (lat, k_rope), (k_sb, v_sb), None
    else:
        p_lat, p_kr, p_k, p_v, h0_re, h0_im = past
        keys_a = (jnp.concatenate([p_lat, lat], axis=1), jnp.concatenate([p_kr, k_rope], axis=1))
        keys_c = (jnp.concatenate([p_k, k_sb], axis=1), jnp.concatenate([p_v, v_sb], axis=1))
        h0 = (h0_re, h0_im)
    o_lat = causal_sweep(mla_attend, (q_lat, q_rope), keys_a, q_start)
    o_a = jnp.einsum('bthr,rhd->bthd', o_lat, w['w_uv'][l]).reshape(bsz, t, MLA_HEADS * V_DIM)
    o_b, h_last_re, h_last_im = s5_branch(
        u, h0, w['ssm_lam_re'][l], w['ssm_lam_im'][l], w['ssm_log_dt'][l], w['ssm_b_re'][l],
        w['ssm_b_im'][l], w['ssm_c_re'][l], w['ssm_c_im'][l], w['ssm_d'][l], w['ssm_w_glu'][l],
        w['ssm_b_glu'][l])
    o_c = causal_sweep(sb_attend, (q_sb,), keys_c, q_start)
    merged = (jax.nn.sigmoid(gate_a) * (o_a @ w['w_br_a'][l])
              + jax.nn.sigmoid(gate_b) * (o_b @ w['w_br_b'][l])
              + jax.nn.sigmoid(gate_c) * (o_c @ w['w_br_c'][l]))
    return merged @ w['w_out'][l], (lat, k_rope, k_sb, v_sb, h_last_re, h_last_im)


def trunk(x, q_start, caches, w):
    rows = []
    for l in range(DEPTH):
        past = None
        if caches is not None:
            lat_pool, kr_pool, k_pool, v_pool, s_re, s_im, page_table = caches
            past = (gather_pages(lat_pool, l, page_table), gather_pages(kr_pool, l, page_table),
                    gather_pages(k_pool, l, page_table), gather_pages(v_pool, l, page_table),
                    s_re[l], s_im[l])
        mix, new = token_mixer(rmsnorm(x, w['g_mix'][l]), q_start, past, l, w)
        x = x + mix
        hf = rmsnorm(x, w['g_ffn'][l])
        i = l // 2
        if l % 2 == 0:
            x = x + swiglu(hf, w['ffn_w_gate'][i], w['ffn_w_up'][i], w['ffn_w_down'][i])
        else:
            x = x + moe_ffn(hf, w['moe_router'][i], w['moe_w_gate'][i], w['moe_w_up'][i],
                            w['moe_w_down'][i])
        rows.append(new)
    stacked = [jnp.stack([r[j] for r in rows]) for j in range(len(rows[0]))]
    return rmsnorm(x, w['g_final']), stacked


def setup_inputs(seed: int = 0) -> dict:
    key = jax.random.key(seed)
    ks = iter(jax.random.split(key, 48))

    def nrm(shape, scale):
        return scale * jax.random.normal(next(ks), shape, jnp.float32)

    def gain(shape):
        return 1.0 + nrm(shape, 0.02)

    n_pages = PAST_LEN // PAGE_SIZE
    n_phys = (DEC_BATCH * n_pages * 5) // 4
    page_table = jax.random.permutation(next(ks), n_phys)[:DEC_BATCH * n_pages]
    page_table = page_table.reshape(DEC_BATCH, n_pages).astype(jnp.int32)
    br = MLA_HEADS * V_DIM
    return {
        'x_prompt': nrm((BATCH, SEQ, D_MODEL), 1.0),
        'x_sample': nrm((DEC_BATCH, DEC_SEQ, D_MODEL), 1.0),
        'cache_mla_latent': nrm((DEPTH, n_phys, PAGE_SIZE, KV_RANK), 1.0),
        'cache_mla_krope': nrm((DEPTH, n_phys, PAGE_SIZE, ROPE_DIM), 1.0),
        'cache_sb_k': nrm((DEPTH, n_phys, PAGE_SIZE, SB_KV_HEADS, SB_HEAD_DIM), 1.0),
        'cache_sb_v': nrm((DEPTH, n_phys, PAGE_SIZE, SB_KV_HEADS, SB_HEAD_DIM), 1.0),
        'state_ssm_re': nrm((DEPTH, DEC_BATCH, SSM_GROUPS, SSM_STATE), 0.3),
        'state_ssm_im': nrm((DEPTH, DEC_BATCH, SSM_GROUPS, SSM_STATE), 0.3),
        'page_table': page_table,
        'g_mix': gain((DEPTH, D_MODEL)),
        'w_in': nrm((DEPTH, D_MODEL, IN_WIDTH), D_MODEL ** -0.5),
        'g_q': gain((DEPTH, Q_RANK)),
        'g_kv': gain((DEPTH, KV_RANK)),
        'w_uq': nrm((DEPTH, Q_RANK, MLA_HEADS * (QK_NOPE + ROPE_DIM)), Q_RANK ** -0.5),
        'w_uk': nrm((DEPTH, KV_RANK, MLA_HEADS, QK_NOPE), KV_RANK ** -0.5),
        'w_uv': nrm((DEPTH, KV_RANK, MLA_HEADS, V_DIM), KV_RANK ** -0.5),
        'ssm_lam_re': -0.5 + nrm((DEPTH, SSM_GROUPS, SSM_STATE), 0.01),
        'ssm_lam_im': jnp.pi * jnp.arange(SSM_STATE, dtype=jnp.float32) + nrm((DEPTH, SSM_GROUPS, SSM_STATE), 0.01),
        'ssm_log_dt': jax.random.uniform(next(ks), (DEPTH, SSM_GROUPS), jnp.float32,
                                         math.log(DT_MIN), math.log(DT_MAX)),
        'ssm_b_re': nrm((DEPTH, SSM_GROUPS, SSM_STATE, SSM_GROUP_SIZE), (2 * SSM_GROUP_SIZE) ** -0.5),
        'ssm_b_im': nrm((DEPTH, SSM_GROUPS, SSM_STATE, SSM_GROUP_SIZE), (2 * SSM_GROUP_SIZE) ** -0.5),
        'ssm_c_re': nrm((DEPTH, SSM_GROUPS, SSM_GROUP_SIZE, SSM_STATE), SSM_STATE ** -0.5),
        'ssm_c_im': nrm((DEPTH, SSM_GROUPS, SSM_GROUP_SIZE, SSM_STATE), SSM_STATE ** -0.5),
        'ssm_d': nrm((DEPTH, SSM_WIDTH), 1.0),
        'ssm_w_glu': nrm((DEPTH, SSM_WIDTH, SSM_WIDTH), SSM_WIDTH ** -0.5),
        'ssm_b_glu': nrm((DEPTH, SSM_WIDTH), 0.01),
        'w_br_a': nrm((DEPTH, br, D_MODEL), br ** -0.5),
        'w_br_b': nrm((DEPTH, SSM_WIDTH, D_MODEL), SSM_WIDTH ** -0.5),
        'w_br_c': nrm((DEPTH, SB_HEADS * SB_HEAD_DIM, D_MODEL), (SB_HEADS * SB_HEAD_DIM) ** -0.5),
        'w_out': nrm((DEPTH, D_MODEL, D_MODEL), D_MODEL ** -0.5),
        'g_ffn': gain((DEPTH, D_MODEL)),
        'ffn_w_gate': nrm((N_DENSE, D_MODEL, D_FF), D_MODEL ** -0.5),
        'ffn_w_up': nrm((N_DENSE, D_MODEL, D_FF), D_MODEL ** -0.5),
        'ffn_w_down': nrm((N_DENSE, D_FF, D_MODEL), D_FF ** -0.5),
        'moe_router': nrm((N_MOE, D_MODEL, N_EXPERTS), D_MODEL ** -0.5),
        'moe_w_gate': nrm((N_MOE, N_EXPERTS, D_MODEL, D_EXPERT), D_MODEL ** -0.5),
        'moe_w_up': nrm((N_MOE, N_EXPERTS, D_MODEL, D_EXPERT), D_MODEL ** -0.5),
        'moe_w_down': nrm((N_MOE, N_EXPERTS, D_EXPERT, D_MODEL), D_EXPERT ** -0.5),
        'g_final': gain((D_MODEL,)),
    }


def reference(x_prompt, x_sample, cache_mla_latent, cache_mla_krope, cache_sb_k, cache_sb_v,
              state_ssm_re, state_ssm_im, page_table, g_mix, w_in, g_q, g_kv, w_uq, w_uk, w_uv,
              ssm_lam_re, ssm_lam_im, ssm_log_dt, ssm_b_re, ssm_b_im, ssm_c_re, ssm_c_im, ssm_d,
              ssm_w_glu, ssm_b_glu, w_br_a, w_br_b, w_br_c, w_out, g_ffn, ffn_w_gate, ffn_w_up,
              ffn_w_down, moe_router, moe_w_gate, moe_w_up, moe_w_down, g_final):
    w = dict(g_mix=g_mix, w_in=w_in, g_q=g_q, g_kv=g_kv, w_uq=w_uq, w_uk=w_uk, w_uv=w_uv,
             ssm_lam_re=ssm_lam_re, ssm_lam_im=ssm_lam_im, ssm_log_dt=ssm_log_dt,
             ssm_b_re=ssm_b_re, ssm_b_im=ssm_b_im, ssm_c_re=ssm_c_re, ssm_c_im=ssm_c_im,
             ssm_d=ssm_d, ssm_w_glu=ssm_w_glu, ssm_b_glu=ssm_b_glu, w_br_a=w_br_a,
             w_br_b=w_br_b, w_br_c=w_br_c, w_out=w_out, g_ffn=g_ffn, ffn_w_gate=ffn_w_gate,
             ffn_w_up=ffn_w_up, ffn_w_down=ffn_w_down, moe_router=moe_router,
             moe_w_gate=moe_w_gate, moe_w_up=moe_w_up, moe_w_down=moe_w_down, g_final=g_final)
    y_prompt, (lat_p, kr_p, k_p, v_p, re_p, im_p) = trunk(x_prompt, 0, None, w)
    caches = (cache_mla_latent, cache_mla_krope, cache_sb_k, cache_sb_v,
              state_ssm_re, state_ssm_im, page_table)
    y_sample, (lat_s, kr_s, k_s, v_s, re_s, im_s) = trunk(x_sample, PAST_LEN, caches, w)
    return (y_prompt, y_sample, lat_p, lat_s, kr_p, kr_s, k_p, k_s, v_p, v_s, re_p, re_s, im_p, im_s)
```

```python
import functools
import math

import numpy as np
import jax
import jax.numpy as jnp
from jax import lax
from jax.experimental import pallas as pl
from jax.experimental.pallas import tpu as pltpu

F32 = jnp.float32
BF = jnp.bfloat16

MLA_HEADS = 8
QK_NOPE = 64
ROPE_DIM = 32
V_DIM = 64
Q_RANK = 256
KV_RANK = 128
ROPE_THETA = 10000.0
MLA_SCALE = (QK_NOPE + ROPE_DIM) ** -0.5
SSM_WIDTH = 512
SSM_GROUP_SIZE = 16
SSM_GROUPS = SSM_WIDTH // SSM_GROUP_SIZE
SSM_STATE = 64
SSM_FLAT = SSM_GROUPS * SSM_STATE
SB_HEADS = 8
SB_KV_HEADS = 2
SB_HEAD_DIM = 64
SB_GROUP = SB_HEADS // SB_KV_HEADS
SB_SCALE = SB_HEAD_DIM ** -0.5
N_EXPERTS = 8
EPS = 1e-6
NEG_BIG = -1e30

LANES = 128
SUBLANES = 8
VMEM_LIMIT_BYTES = 56 * 1024 * 1024

ROPE_PAD = LANES - ROPE_DIM
SEG_CQ = (0, 256)
SEG_CKV = (256, 384)
SEG_KROPE = (384, 512)
SEG_U = (512, 1024)
SEG_QSB = (1024, 1536)
SEG_KSB = (1536, 1664)
SEG_VSB = (1664, 1792)
SEG_GATES = (1792, 4864)
IN_PADDED = 4864

PAGE_KEYS = 128
SB_KEY_BLOCK = 256
MLA_KEY_BLOCK = 256
Q_POS_BLOCK = 128


def _params(*sem):
    return pltpu.CompilerParams(dimension_semantics=sem, vmem_limit_bytes=VMEM_LIMIT_BYTES)


def _pick_tile(n, pref):
    t = min(n, pref)
    while n % t:
        t -= SUBLANES
    return t


def _rms(x, g):
    return x * lax.rsqrt(jnp.mean(x * x, axis=-1, keepdims=True) + EPS) * g


def _dot(a, b):
    return jnp.dot(a, b, preferred_element_type=F32)


def _dot_nt(a, b):
    return lax.dot_general(a, b, (((1,), (1,)), ((), ())), preferred_element_type=F32)


def _const_spec(shape):
    nd = len(shape)
    return pl.BlockSpec(shape, lambda *_: (0,) * nd)


def _in_proj_kernel(x_ref, gmix_ref, win_ref, gq_ref, gkv_ref, wuq_ref, wukbd_ref, perm_ref,
                    cos8_ref, sin8_ref, ka_ref, kb_ref, kc_ref,
                    qcat_ref, kcat_ref, lat_ref, kr_ref, u_ref, qsb0_ref, qsb1_ref,
                    ksb_ref, vsb_ref, gates_ref):
    h = _rms(x_ref[...], gmix_ref[...]).astype(BF)

    def seg(bounds):
        return _dot(h, win_ref[:, bounds[0]:bounds[1]])

    qn = _rms(seg(SEG_CQ), gq_ref[...]).astype(BF)
    q = _dot(qn, wuq_ref[...])
    n_nope = MLA_HEADS * QK_NOPE
    x1 = q[:, n_nope:n_nope + LANES]
    x2 = q[:, n_nope + LANES:n_nope + 2 * LANES]
    c8 = cos8_ref[...]
    s8 = sin8_ref[...]
    rot = jnp.concatenate([x1 * c8 - x2 * s8, x1 * s8 + x2 * c8], axis=1).astype(BF)
    qr = _dot(rot, perm_ref[...])
    for pair in range(MLA_HEADS // 2):
        qn_pair = q[:, LANES * pair:LANES * (pair + 1)].astype(BF)
        ql = _dot(qn_pair, wukbd_ref[pair])
        for j in range(2):
            hd = 2 * pair + j
            qcat_ref[:, 256 * hd:256 * hd + LANES] = ql[:, LANES * j:LANES * (j + 1)].astype(BF)
            qcat_ref[:, 256 * hd + LANES:256 * (hd + 1)] = qr[:, LANES * hd:LANES * (hd + 1)].astype(BF)

    lat = _rms(seg(SEG_CKV), gkv_ref[...])
    lat_ref[...] = lat
    zk = seg(SEG_KROPE)
    half = ROPE_DIM // 2
    kr = (zk * ka_ref[...] + pltpu.roll(zk, LANES - half, 1) * kb_ref[...]
          + pltpu.roll(zk, half, 1) * kc_ref[...])
    kr_ref[...] = kr[:, :ROPE_DIM]
    kcat_ref[:, :LANES] = lat.astype(BF)
    kcat_ref[:, LANES:] = kr.astype(BF)

    u_ref[...] = seg(SEG_U)
    qs = seg(SEG_QSB) * SB_SCALE
    hw = SB_GROUP * SB_HEAD_DIM
    qsb0_ref[...] = qs[:, :hw].astype(BF)
    qsb1_ref[...] = qs[:, hw:].astype(BF)
    ksb_ref[...] = seg(SEG_KSB)
    vsb_ref[...] = seg(SEG_VSB)
    g0 = SEG_GATES[0]
    d = (SEG_GATES[1] - g0) // 3
    for j in range(3):
        gates_ref[:, d * j:d * (j + 1)] = jax.nn.sigmoid(seg((g0 + d * j, g0 + d * (j + 1)))).astype(BF)


def _in_proj(x, lw, tabs, tm):
    n, d = x.shape
    nblk = tabs[0].shape[0] // tm
    row = lambda w: pl.BlockSpec((tm, w), lambda i: (i, 0))
    tab = pl.BlockSpec((tm, LANES), lambda i: (i % nblk, 0))
    hw = SB_GROUP * SB_HEAD_DIM
    kvw = SB_KV_HEADS * SB_HEAD_DIM
    out_shape = (
        jax.ShapeDtypeStruct((n, MLA_HEADS * 256), BF),
        jax.ShapeDtypeStruct((n, 256), BF),
        jax.ShapeDtypeStruct((n, KV_RANK), F32),
        jax.ShapeDtypeStruct((n, ROPE_DIM), F32),
        jax.ShapeDtypeStruct((n, SSM_WIDTH), F32),
        jax.ShapeDtypeStruct((n, hw), BF),
        jax.ShapeDtypeStruct((n, hw), BF),
        jax.ShapeDtypeStruct((n, kvw), F32),
        jax.ShapeDtypeStruct((n, kvw), F32),
        jax.ShapeDtypeStruct((n, 3 * d), BF),
    )
    out_specs = tuple(row(s.shape[1]) for s in out_shape)
    consts = [lw['g_mix'], lw['w_in_p'], lw['g_q'], lw['g_kv'], lw['w_uq_p'], lw['w_uk_bd'], lw['perm']]
    return pl.pallas_call(
        _in_proj_kernel,
        grid=(n // tm,),
        in_specs=[row(d)] + [_const_spec(c.shape) for c in consts] + [tab] * 5,
        out_specs=out_specs,
        out_shape=out_shape,
        compiler_params=_params("parallel"),
        name="in_proj",
    )(x, *consts, *tabs)


def _mla_prompt_kernel(qi_ref, kj_ref, last_ref, q_ref, k_ref, o_ref, m_sc, acc_sc, *, tq, kb):
    p = pl.program_id(1)
    qi = qi_ref[p]
    kj = kj_ref[p]

    @pl.when(kj == 0)
    def _():
        m_sc[...] = jnp.full(m_sc.shape, NEG_BIG, F32)
        acc_sc[...] = jnp.zeros(acc_sc.shape, F32)

    q = q_ref[...]
    k = k_ref[...]
    s = _dot_nt(q, k) * MLA_SCALE
    row = lax.broadcasted_iota(jnp.int32, s.shape, 0)
    col = lax.broadcasted_iota(jnp.int32, s.shape, 1)
    q_pos = qi * tq + row // MLA_HEADS
    k_pos = kj * kb + col
    s = jnp.where(k_pos <= q_pos, s, NEG_BIG)
    m_prev = m_sc[...]
    m_new = jnp.maximum(m_prev, jnp.max(s, axis=1, keepdims=True))
    alpha = jnp.exp(m_prev - m_new)
    pr = jnp.exp(s - m_new).astype(BF)
    v_ext = jnp.concatenate([k[:, :KV_RANK], jnp.ones((kb, LANES), BF)], axis=1)
    acc_sc[...] = alpha * acc_sc[...] + _dot(pr, v_ext)
    m_sc[...] = m_new

    @pl.when(last_ref[p] == 1)
    def _():
        acc = acc_sc[...]
        o_ref[...] = (acc[:, :KV_RANK] / acc[:, KV_RANK:KV_RANK + 1]).astype(BF)


def _causal_pairs(t, tq, kb, strict, descending):
    qi_l, kj_l, first_l, last_l = [], [], [], []
    for qi in range(t // tq):
        last_key = qi * tq + tq - 1 - (1 if strict else 0)
        ks = list(range(last_key // kb + 1))
        if descending:
            ks = ks[::-1]
        for n, kj in enumerate(ks):
            qi_l.append(qi)
            kj_l.append(kj)
            first_l.append(1 if n == 0 else 0)
            last_l.append(1 if n == len(ks) - 1 else 0)
    as_arr = lambda v: jnp.asarray(np.asarray(v, np.int32))
    return as_arr(qi_l), as_arr(kj_l), as_arr(first_l), as_arr(last_l)


def _mla_prompt(qcat, kcat, bsz, t):
    tq = min(Q_POS_BLOCK, t)
    kb = min(MLA_KEY_BLOCK, t)
    qi_t, kj_t, _, last_t = _causal_pairs(t, tq, kb, strict=False, descending=False)
    rows = tq * MLA_HEADS
    q2 = qcat.reshape(bsz * t * MLA_HEADS, 256)
    nq, nk = t // tq, t // kb
    grid_spec = pltpu.PrefetchScalarGridSpec(
        num_scalar_prefetch=3,
        grid=(bsz, int(qi_t.shape[0])),
        in_specs=[
            pl.BlockSpec((rows, 256), lambda b, p, qi, kj, la: (b * nq + qi[p], 0)),
            pl.BlockSpec((kb, 256), lambda b, p, qi, kj, la: (b * nk + kj[p], 0)),
        ],
        out_specs=pl.BlockSpec((rows, KV_RANK), lambda b, p, qi, kj, la: (b * nq + qi[p], 0)),
        scratch_shapes=[pltpu.VMEM((rows, 1), F32), pltpu.VMEM((rows, 2 * LANES), F32)],
    )
    o = pl.pallas_call(
        functools.partial(_mla_prompt_kernel, tq=tq, kb=kb),
        grid_spec=grid_spec,
        out_shape=jax.ShapeDtypeStruct((bsz * t * MLA_HEADS, KV_RANK), BF),
        compiler_params=_params("parallel", "arbitrary"),
        name="mla_prompt",
    )(qi_t, kj_t, last_t, q2, kcat)
    return o.reshape(bsz * t, MLA_HEADS * KV_RANK)


def _mla_sample_kernel(pt_ref, q_ref, newk_ref, *refs, pb, t_new):
    lat_refs = refs[:pb]
    kr_refs = refs[pb:2 * pb]
    o_ref, m_sc, l_sc, acc_sc = refs[2 * pb:]
    j = pl.program_id(1)
    q = q_ref[...]

    def update(s, vals):
        m_prev = m_sc[...]
        m_new = jnp.maximum(m_prev, jnp.max(s, axis=1, keepdims=True))
        alpha = jnp.exp(m_prev - m_new)
        pr = jnp.exp(s - m_new)
        l_sc[...] = alpha * l_sc[...] + jnp.sum(pr, axis=1, keepdims=True)
        prb = pr.astype(BF)
        pv = _dot(prb[:, :PAGE_KEYS], vals[0])
        for i in range(1, len(vals)):
            pv = pv + _dot(prb[:, PAGE_KEYS * i:PAGE_KEYS * (i + 1)], vals[i])
        acc_sc[...] = alpha * acc_sc[...] + pv
        m_sc[...] = m_new

    @pl.when(j == 0)
    def _():
        m_sc[...] = jnp.full(m_sc.shape, NEG_BIG, F32)
        l_sc[...] = jnp.zeros(l_sc.shape, F32)
        acc_sc[...] = jnp.zeros(acc_sc.shape, F32)
        nk = newk_ref[...]
        s = _dot_nt(q, nk) * MLA_SCALE
        row = lax.broadcasted_iota(jnp.int32, s.shape, 0)
        col = lax.broadcasted_iota(jnp.int32, s.shape, 1)
        s = jnp.where(col <= row // MLA_HEADS, s, NEG_BIG)
        update(s, [nk[:, :KV_RANK]])

    @pl.when(j > 0)
    def _():
        q_lat = q[:, :KV_RANK]
        q_rope = q[:, KV_RANK:KV_RANK + ROPE_DIM]
        vals, scores = [], []
        for i in range(pb):
            lat = lat_refs[i][...].astype(BF)
            kr = kr_refs[i][...].astype(BF)
            scores.append(_dot_nt(q_lat, lat) + _dot_nt(q_rope, kr))
            vals.append(lat)
        update(jnp.concatenate(scores, axis=1) * MLA_SCALE, vals)

    @pl.when(j == pl.num_programs(1) - 1)
    def _():
        o_ref[...] = (acc_sc[...] / l_sc[...]).astype(BF)


def _mla_sample(qcat, kcat, lat_pool, kr_pool, page_table, layer, bsz, t):
    n_pages = page_table.shape[1]
    pb = min(16, n_pages)
    nch = n_pages // pb
    rows = t * MLA_HEADS
    q3 = qcat.reshape(bsz, rows, 256)
    newk = jnp.pad(kcat.reshape(bsz, t, 256), ((0, 0), (0, PAGE_KEYS - t), (0, 0)))

    def page_spec(width, i):
        return pl.BlockSpec(
            (None, None, PAGE_KEYS, width),
            lambda b, j, pt: (layer, pt[b, jnp.maximum(j - 1, 0) * pb + i], 0, 0))

    grid_spec = pltpu.PrefetchScalarGridSpec(
        num_scalar_prefetch=1,
        grid=(bsz, nch + 1),
        in_specs=[pl.BlockSpec((None, rows, 256), lambda b, j, pt: (b, 0, 0)),
                  pl.BlockSpec((None, PAGE_KEYS, 256), lambda b, j, pt: (b, 0, 0))]
                 + [page_spec(KV_RANK, i) for i in range(pb)]
                 + [page_spec(ROPE_DIM, i) for i in range(pb)],
        out_specs=pl.BlockSpec((None, rows, KV_RANK), lambda b, j, pt: (b, 0, 0)),
        scratch_shapes=[pltpu.VMEM((rows, 1), F32), pltpu.VMEM((rows, 1), F32),
                        pltpu.VMEM((rows, KV_RANK), F32)],
    )
    o = pl.pallas_call(
        functools.partial(_mla_sample_kernel, pb=pb, t_new=t),
        grid_spec=grid_spec,
        out_shape=jax.ShapeDtypeStruct((bsz, rows, KV_RANK), BF),
        compiler_params=_params("parallel", "arbitrary"),
        name="mla_sample",
    )(page_table, q3, newk, *([lat_pool] * pb), *([kr_pool] * pb))
    return o.reshape(bsz * t, MLA_HEADS * KV_RANK)


def _sb_scores(q, k, mask):
    z = _dot_nt(q, k)
    log_1m = -(jnp.maximum(z, 0.0) + jnp.log1p(jnp.exp(-jnp.abs(z))))
    if mask is not None:
        log_1m = jnp.where(mask, log_1m, 0.0)
    return z, log_1m


def _suffix_sums(log_1m, tri):
    hi = log_1m.astype(BF)
    lo = (log_1m - hi.astype(F32)).astype(BF)
    return _dot(hi, tri) + _dot(lo, tri)


def _sb_weights(z, log_1m, between, mask):
    a = jnp.exp(z + log_1m + between)
    if mask is not None:
        a = jnp.where(mask, a, 0.0)
    return a.astype(BF)


def _sb_prompt_kernel(qi_ref, kj_ref, first_ref, last_ref, q0_ref, q1_ref, k_ref, v_ref, tri_ref,
                      o0_ref, o1_ref, carry_sc, acc_sc, *, tq, kb):
    p = pl.program_id(1)
    qi = qi_ref[p]
    kj = kj_ref[p]

    @pl.when(first_ref[p] == 1)
    def _():
        carry_sc[...] = jnp.zeros(carry_sc.shape, F32)
        acc_sc[...] = jnp.zeros(acc_sc.shape, F32)

    rows = tq * SB_GROUP
    row = lax.broadcasted_iota(jnp.int32, (rows, kb), 0)
    col = lax.broadcasted_iota(jnp.int32, (rows, kb), 1)
    mask = (kj * kb + col) < (qi * tq + row // SB_GROUP)
    tri = tri_ref[...]
    kf = k_ref[...]
    vf = v_ref[...]
    for kv, q_ref in enumerate((q0_ref, q1_ref)):
        k = kf[:, SB_HEAD_DIM * kv:SB_HEAD_DIM * (kv + 1)].astype(BF)
        v = vf[:, SB_HEAD_DIM * kv:SB_HEAD_DIM * (kv + 1)].astype(BF)
        z, log_1m = _sb_scores(q_ref[...], k, mask)
        w = _suffix_sums(log_1m, tri)
        carry = carry_sc[kv]
        a = _sb_weights(z, log_1m, carry + w, mask)
        acc_sc[kv] = acc_sc[kv] + _dot(a, v)
        carry_sc[kv] = carry + (w + log_1m)[:, 0:1]

    @pl.when(last_ref[p] == 1)
    def _():
        o0_ref[...] = acc_sc[0].astype(BF)
        o1_ref[...] = acc_sc[1].astype(BF)


def _strict_lower_ones(n):
    j = np.arange(n)[:, None]
    s = np.arange(n)[None, :]
    return jnp.asarray((j > s).astype(np.float32), BF)


def _sb_prompt(qsb0, qsb1, ksb, vsb, bsz, t):
    tq = min(Q_POS_BLOCK, t)
    kb = min(SB_KEY_BLOCK, t)
    qi_t, kj_t, first_t, last_t = _causal_pairs(t, tq, kb, strict=True, descending=True)
    rows = tq * SB_GROUP
    nq, nk = t // tq, t // kb
    n = bsz * t
    q0 = qsb0.reshape(n * SB_GROUP, SB_HEAD_DIM)
    q1 = qsb1.reshape(n * SB_GROUP, SB_HEAD_DIM)
    kvw = SB_KV_HEADS * SB_HEAD_DIM
    qspec = pl.BlockSpec((rows, SB_HEAD_DIM), lambda b, p, qi, kj, fi, la: (b * nq + qi[p], 0))
    kspec = pl.BlockSpec((kb, kvw), lambda b, p, qi, kj, fi, la: (b * nk + kj[p], 0))
    grid_spec = pltpu.PrefetchScalarGridSpec(
        num_scalar_prefetch=4,
        grid=(bsz, int(qi_t.shape[0])),
        in_specs=[qspec, qspec, kspec, kspec,
                  pl.BlockSpec((kb, kb), lambda b, p, qi, kj, fi, la: (0, 0))],
        out_specs=(qspec, qspec),
        scratch_shapes=[pltpu.VMEM((SB_KV_HEADS, rows, 1), F32),
                        pltpu.VMEM((SB_KV_HEADS, rows, SB_HEAD_DIM), F32)],
    )
    o_sds = jax.ShapeDtypeStruct((n * SB_GROUP, SB_HEAD_DIM), BF)
    o0, o1 = pl.pallas_call(
        functools.partial(_sb_prompt_kernel, tq=tq, kb=kb),
        grid_spec=grid_spec,
        out_shape=(o_sds, o_sds),
        compiler_params=_params("parallel", "arbitrary"),
        name="sb_prompt",
    )(qi_t, kj_t, first_t, last_t, q0, q1, ksb, vsb, _strict_lower_ones(kb))
    hw = SB_GROUP * SB_HEAD_DIM
    return o0.reshape(n, hw), o1.reshape(n, hw)


def _sb_sample_kernel(pt_ref, q_ref, newk_ref, newv_ref, tri_new_ref, tri_ref, *refs, pb, t_new):
    k_refs = refs[:pb]
    v_refs = refs[pb:2 * pb]
    o_ref, carry_sc, acc_sc = refs[2 * pb:]
    j = pl.program_id(1)
    q = q_ref[...]
    rows_per_kv = t_new * SB_GROUP

    @pl.when(j == 0)
    def _():
        nk = newk_ref[...].astype(BF)
        nv = newv_ref[...].astype(BF)
        row = lax.broadcasted_iota(jnp.int32, (q.shape[0], PAGE_KEYS), 0)
        col = lax.broadcasted_iota(jnp.int32, (q.shape[0], PAGE_KEYS), 1)
        mask = col < (row % rows_per_kv) // SB_GROUP
        z, log_1m = _sb_scores(q, nk, mask)
        w = _suffix_sums(log_1m, tri_new_ref[...])
        a = _sb_weights(z, log_1m, w, mask)
        acc_sc[...] = _dot(a, nv)
        carry_sc[...] = (w + log_1m)[:, 0:1]

    @pl.when(j > 0)
    def _():
        tri = tri_ref[...]
        nsub = pb // 2
        zs, ls, ws, vs = [], [], [], []
        for sb in range(nsub):
            k = jnp.concatenate([k_refs[2 * sb][...], k_refs[2 * sb + 1][...]], axis=0).astype(BF)
            v = jnp.concatenate([v_refs[2 * sb][...], v_refs[2 * sb + 1][...]], axis=0).astype(BF)
            z, log_1m = _sb_scores(q, k, None)
            zs.append(z)
            ls.append(log_1m)
            ws.append(_suffix_sums(log_1m, tri))
            vs.append(v)
        carry = carry_sc[...]
        acc = acc_sc[...]
        for sb in reversed(range(nsub)):
            a = _sb_weights(zs[sb], ls[sb], carry + ws[sb], None)
            acc = acc + _dot(a, vs[sb])
            carry = carry + (ws[sb] + ls[sb])[:, 0:1]
        carry_sc[...] = carry
        acc_sc[...] = acc

    @pl.when(j == pl.num_programs(1) - 1)
    def _():
        o_ref[...] = acc_sc[...]


def _sb_sample(qsb0, qsb1, ksb, vsb, k_pool, v_pool, page_table, layer, bsz, t):
    n_pages = page_table.shape[1]
    pb = min(16, n_pages)
    nch = n_pages // pb
    kvw = SB_KV_HEADS * SB_HEAD_DIM
    rows = SB_KV_HEADS * t * SB_GROUP
    q0 = qsb0.reshape(bsz, t * SB_GROUP, SB_HEAD_DIM)
    q1 = qsb1.reshape(bsz, t * SB_GROUP, SB_HEAD_DIM)
    zq = jnp.zeros_like(q0)
    q = jnp.concatenate([jnp.concatenate([q0, zq], axis=2), jnp.concatenate([zq, q1], axis=2)], axis=1)
    pad = ((0, 0), (0, PAGE_KEYS - t), (0, 0))
    newk = jnp.pad(ksb.reshape(bsz, t, kvw), pad)
    newv = jnp.pad(vsb.reshape(bsz, t, kvw), pad)
    depth, n_phys = k_pool.shape[0], k_pool.shape[1]
    kp = k_pool.reshape(depth, n_phys, PAGE_KEYS, kvw)
    vp = v_pool.reshape(depth, n_phys, PAGE_KEYS, kvw)

    def page_spec(i):
        return pl.BlockSpec(
            (None, None, PAGE_KEYS, kvw),
            lambda b, j, pt: (layer, pt[b, (nch - jnp.maximum(j, 1)) * pb + i], 0, 0))

    per_b = lambda r, w: pl.BlockSpec((None, r, w), lambda b, j, pt: (b, 0, 0))
    const2 = lambda r: pl.BlockSpec((r, r), lambda b, j, pt: (0, 0))
    grid_spec = pltpu.PrefetchScalarGridSpec(
        num_scalar_prefetch=1,
        grid=(bsz, nch + 1),
        in_specs=[per_b(rows, kvw), per_b(PAGE_KEYS, kvw), per_b(PAGE_KEYS, kvw),
                  const2(PAGE_KEYS), const2(2 * PAGE_KEYS)]
                 + [page_spec(i) for i in range(pb)] * 2,
        out_specs=per_b(rows, kvw),
        scratch_shapes=[pltpu.VMEM((rows, 1), F32), pltpu.VMEM((rows, kvw), F32)],
    )
    o = pl.pallas_call(
        functools.partial(_sb_sample_kernel, pb=pb, t_new=t),
        grid_spec=grid_spec,
        out_shape=jax.ShapeDtypeStruct((bsz, rows, kvw), F32),
        compiler_params=_params("parallel", "arbitrary"),
        name="sb_sample",
    )(page_table, q, newk, newv, _strict_lower_ones(PAGE_KEYS), _strict_lower_ones(2 * PAGE_KEYS),
      *([kp] * pb), *([vp] * pb))
    o = o.reshape(bsz, SB_KV_HEADS, t * SB_GROUP, SB_KV_HEADS, SB_HEAD_DIM)
    hw = SB_GROUP * SB_HEAD_DIM
    o0 = o[:, 0, :, 0, :].reshape(bsz * t, hw).astype(BF)
    o1 = o[:, 1, :, 1, :].reshape(bsz * t, hw).astype(BF)
    return o0, o1


def _s5_kernel(u_ref, h0re_ref, h0im_ref, are_ref, aim_ref, bbd_ref, cbd_ref, dskip_ref, wglu_ref,
               bglu_ref, ob_ref, hre_ref, him_ref, s_sc, st_sc, *, bn, tc, lc):
    c = pl.program_id(0)

    @pl.when(c == 0)
    def _():
        st_sc[0] = h0re_ref[...]
        st_sc[1] = h0im_ref[...]

    u = u_ref[...]
    ub = u.astype(BF)
    nflat = SSM_FLAT
    for jb in range(2 * nflat // 512):
        s_sc[:, 512 * jb:512 * (jb + 1)] = _dot(ub, bbd_ref[:, 512 * jb:512 * (jb + 1)])

    for ci in range(nflat // lc):
        lo = ci * lc
        ar = jnp.broadcast_to(are_ref[:, lo:lo + lc], (bn, lc))
        ai = jnp.broadcast_to(aim_ref[:, lo:lo + lc], (bn, lc))

        def body(t, carry, lo=lo, ar=ar, ai=ai):
            hr, hi = carry
            r0 = pl.multiple_of(t * bn, bn)
            br = s_sc[pl.ds(r0, bn), lo:lo + lc]
            bi = s_sc[pl.ds(r0, bn), nflat + lo:nflat + lo + lc]
            nr = ar * hr - ai * hi + br
            ni = ar * hi + ai * hr + bi
            s_sc[pl.ds(r0, bn), lo:lo + lc] = nr
            s_sc[pl.ds(r0, bn), nflat + lo:nflat + lo + lc] = ni
            return nr, ni

        hr, hi = lax.fori_loop(0, tc, body, (st_sc[0, :, lo:lo + lc], st_sc[1, :, lo:lo + lc]))
        st_sc[0, :, lo:lo + lc] = hr
        st_sc[1, :, lo:lo + lc] = hi

    y = dskip_ref[...] * u
    for jb in range(2 * nflat // 1024):
        y = y + _dot(s_sc[:, 1024 * jb:1024 * (jb + 1)].astype(BF), cbd_ref[1024 * jb:1024 * (jb + 1), :])
    g = jax.nn.gelu(y)
    gate = jax.nn.sigmoid(_dot(g.astype(BF), wglu_ref[...]) + bglu_ref[...])
    ob_ref[...] = (g * gate).astype(BF)

    @pl.when(c == pl.num_programs(0) - 1)
    def _():
        hre_ref[...] = st_sc[0]
        him_ref[...] = st_sc[1]


def _s5(u_tm, h0_re, h0_im, lw, bn, t):
    tc = t
    while tc * bn > 512 and tc % 2 == 0:
        tc //= 2
    rows = tc * bn
    lc = 256 if bn <= 16 else LANES
    nflat = SSM_FLAT
    consts = [lw['ssm_a_re'], lw['ssm_a_im'], lw['ssm_bbd'], lw['ssm_cbd'], lw['ssm_d'],
              lw['ssm_w_glu'], lw['ssm_b_glu']]
    st_spec = pl.BlockSpec((bn, nflat), lambda c: (0, 0))
    st_sds = jax.ShapeDtypeStruct((bn, nflat), F32)
    return pl.pallas_call(
        functools.partial(_s5_kernel, bn=bn, tc=tc, lc=lc),
        grid=(t // tc,),
        in_specs=[pl.BlockSpec((rows, SSM_WIDTH), lambda c: (c, 0)), st_spec, st_spec]
                 + [_const_spec(a.shape) for a in consts],
        out_specs=(pl.BlockSpec((rows, SSM_WIDTH), lambda c: (c, 0)), st_spec, st_spec),
        out_shape=(jax.ShapeDtypeStruct((t * bn, SSM_WIDTH), BF), st_sds, st_sds),
        scratch_shapes=[pltpu.VMEM((rows, 2 * nflat), F32), pltpu.VMEM((2, bn, nflat), F32)],
        compiler_params=_params("arbitrary"),
        name="s5",
    )(u_tm, h0_re, h0_im, *consts)


def _merge_kernel(x_ref, olat_ref, ob_ref, oc0_ref, oc1_ref, gates_ref, wuv_ref, wa_ref, wb_ref,
                  wc_ref, wout_ref, gffn_ref, *rest, routed):
    if routed:
        router_ref, x1_ref, hf_ref, gate_ref = rest
    else:
        x1_ref, hf_ref = rest
    d = x_ref.shape[1]
    hw = SB_GROUP * SB_HEAD_DIM
    o_a = _dot(olat_ref[...], wuv_ref[...]).astype(BF)
    br_a = _dot(o_a, wa_ref[...])
    br_b = _dot(ob_ref[...], wb_ref[...])
    br_c = _dot(oc0_ref[...], wc_ref[:hw, :]) + _dot(oc1_ref[...], wc_ref[hw:, :])
    merged = (gates_ref[:, :d].astype(F32) * br_a + gates_ref[:, d:2 * d].astype(F32) * br_b
              + gates_ref[:, 2 * d:].astype(F32) * br_c)
    x1 = x_ref[...] + _dot(merged.astype(BF), wout_ref[...])
    x1_ref[...] = x1
    hf = _rms(x1, gffn_ref[...])
    hf_ref[...] = hf.astype(BF)
    if routed:
        logits = jnp.dot(hf, router_ref[...], preferred_element_type=F32, precision=lax.Precision.HIGHEST)
        lane = lax.broadcasted_iota(jnp.int32, logits.shape, 1)
        lg = jnp.where(lane < N_EXPERTS, logits, NEG_BIG)
        v1 = jnp.max(lg, axis=1, keepdims=True)
        i1 = jnp.min(jnp.where(lg == v1, lane, LANES), axis=1, keepdims=True)
        lg2 = jnp.where(lane == i1, NEG_BIG, lg)
        v2 = jnp.max(lg2, axis=1, keepdims=True)
        i2 = jnp.min(jnp.where(lg2 == v2, lane, LANES), axis=1, keepdims=True)
        e2 = jnp.exp(v2 - v1)
        p1 = 1.0 / (1.0 + e2)
        p2 = e2 / (1.0 + e2)
        gate_ref[...] = jnp.where(lane == i1, p1, 0.0) + jnp.where(lane == i2, p2, 0.0)


def _merge(x, olat, ob, oc0, oc1, gates, lw, router, tm):
    n, d = x.shape
    routed = router is not None
    row = lambda w: pl.BlockSpec((tm, w), lambda i: (i, 0))
    consts = [lw['w_uv_bd'], lw['w_br_a'], lw['w_br_b'], lw['w_br_c'], lw['w_out'], lw['g_ffn']]
    if routed:
        consts.append(router)
    acts = [x, olat, ob, oc0, oc1, gates]
    out_shape = [jax.ShapeDtypeStruct((n, d), F32), jax.ShapeDtypeStruct((n, d), BF)]
    if routed:
        out_shape.append(jax.ShapeDtypeStruct((n, LANES), F32))
    return pl.pallas_call(
        functools.partial(_merge_kernel, routed=routed),
        grid=(n // tm,),
        in_specs=[row(a.shape[1]) for a in acts] + [_const_spec(c.shape) for c in consts],
        out_specs=tuple(row(s.shape[1]) for s in out_shape),
        out_shape=tuple(out_shape),
        compiler_params=_params("parallel"),
        name="merge",
    )(*acts, *consts)


def _ffn_kernel(h_ref, x_ref, wg_ref, wu_ref, wd_ref, o_ref, acc_sc):
    f = pl.program_id(1)

    @pl.when(f == 0)
    def _():
        acc_sc[...] = jnp.zeros(acc_sc.shape, F32)

    h = h_ref[...]
    act = (jax.nn.silu(_dot(h, wg_ref[...])) * _dot(h, wu_ref[...])).astype(BF)
    acc_sc[...] += _dot(act, wd_ref[...])

    @pl.when(f == pl.num_programs(1) - 1)
    def _():
        o_ref[...] = x_ref[...] + acc_sc[...]


def _ffn(hf, x, wg, wu, wd, tm, tf):
    n, d = x.shape
    dff = wg.shape[1]
    return pl.pallas_call(
        _ffn_kernel,
        grid=(n // tm, dff // tf),
        in_specs=[pl.BlockSpec((tm, d), lambda i, f: (i, 0)),
                  pl.BlockSpec((tm, d), lambda i, f: (i, 0)),
                  pl.BlockSpec((d, tf), lambda i, f: (0, f)),
                  pl.BlockSpec((d, tf), lambda i, f: (0, f)),
                  pl.BlockSpec((tf, d), lambda i, f: (f, 0))],
        out_specs=pl.BlockSpec((tm, d), lambda i, f: (i, 0)),
        out_shape=jax.ShapeDtypeStruct((n, d), F32),
        scratch_shapes=[pltpu.VMEM((tm, d), F32)],
        compiler_params=_params("parallel", "arbitrary"),
        name="ffn_dense",
    )(hf, x, wg, wu, wd)


def _moe_kernel(h_ref, x_ref, gate_ref, wg_ref, wu_ref, wd_ref, o_ref, acc_sc):
    e = pl.program_id(1)
    f = pl.program_id(2)

    @pl.when((e == 0) & (f == 0))
    def _():
        acc_sc[...] = jnp.zeros(acc_sc.shape, F32)

    h = h_ref[...]
    act = (jax.nn.silu(_dot(h, wg_ref[...])) * _dot(h, wu_ref[...])).astype(BF)
    gate = gate_ref[...]
    lane = lax.broadcasted_iota(jnp.int32, gate.shape, 1)
    g_e = jnp.sum(jnp.where(lane == e, gate, 0.0), axis=1, keepdims=True)
    acc_sc[...] += g_e * _dot(act, wd_ref[...])

    @pl.when((e == pl.num_programs(1) - 1) & (f == pl.num_programs(2) - 1))
    def _():
        o_ref[...] = x_ref[...] + acc_sc[...]


def _moe(hf, x, gate, wg, wu, wd, tm, tf):
    n, d = x.shape
    ne, _, dff = wg.shape
    return pl.pallas_call(
        _moe_kernel,
        grid=(n // tm, ne, dff // tf),
        in_specs=[pl.BlockSpec((tm, d), lambda i, e, f: (i, 0)),
                  pl.BlockSpec((tm, d), lambda i, e, f: (i, 0)),
                  pl.BlockSpec((tm, LANES), lambda i, e, f: (i, 0)),
                  pl.BlockSpec((None, d, tf), lambda i, e, f: (e, 0, f)),
                  pl.BlockSpec((None, d, tf), lambda i, e, f: (e, 0, f)),
                  pl.BlockSpec((None, tf, d), lambda i, e, f: (e, f, 0))],
        out_specs=pl.BlockSpec((tm, d), lambda i, e, f: (i, 0)),
        out_shape=jax.ShapeDtypeStruct((n, d), F32),
        scratch_shapes=[pltpu.VMEM((tm, d), F32)],
        compiler_params=_params("parallel", "arbitrary", "arbitrary"),
        name="moe",
    )(hf, x, gate, wg, wu, wd)


def _final_norm_kernel(x_ref, g_ref, o_ref):
    o_ref[...] = _rms(x_ref[...], g_ref[...])


def _final_norm(x, g, tm):
    n, d = x.shape
    return pl.pallas_call(
        _final_norm_kernel,
        grid=(n // tm,),
        in_specs=[pl.BlockSpec((tm, d), lambda i: (i, 0)), _const_spec(g.shape)],
        out_specs=pl.BlockSpec((tm, d), lambda i: (i, 0)),
        out_shape=jax.ShapeDtypeStruct((n, d), F32),
        compiler_params=_params("parallel"),
        name="final_norm",
    )(x, g)


def _prep_layer(w, l):
    d = w['w_in'].shape[1]
    lw = {}
    lw['g_mix'] = w['g_mix'][l][None, :]
    w_in = w['w_in'][l]
    split = SEG_KROPE[0] + ROPE_DIM
    lw['w_in_p'] = jnp.concatenate(
        [w_in[:, :split], jnp.zeros((d, ROPE_PAD), w_in.dtype), w_in[:, split:]], axis=1).astype(BF)
    lw['g_q'] = w['g_q'][l][None, :]
    lw['g_kv'] = w['g_kv'][l][None, :]
    half = ROPE_DIM // 2
    w3 = w['w_uq'][l].reshape(Q_RANK, MLA_HEADS, QK_NOPE + ROPE_DIM)
    lw['w_uq_p'] = jnp.concatenate(
        [w3[:, :, :QK_NOPE].reshape(Q_RANK, -1),
         w3[:, :, QK_NOPE:QK_NOPE + half].reshape(Q_RANK, -1),
         w3[:, :, QK_NOPE + half:].reshape(Q_RANK, -1)], axis=1).astype(BF)
    wk = jnp.transpose(w['w_uk'][l], (1, 2, 0))
    zk = jnp.zeros((QK_NOPE, KV_RANK), wk.dtype)
    lw['w_uk_bd'] = jnp.stack([
        jnp.concatenate([jnp.concatenate([wk[2 * p], zk], axis=1),
                         jnp.concatenate([zk, wk[2 * p + 1]], axis=1)], axis=0)
        for p in range(MLA_HEADS // 2)]).astype(BF)
    perm = np.zeros((2 * LANES, MLA_HEADS * LANES), np.float32)
    for h in range(MLA_HEADS):
        for i in range(half):
            perm[h * half + i, LANES * h + i] = 1.0
            perm[LANES + h * half + i, LANES * h + half + i] = 1.0
    lw['perm'] = jnp.asarray(perm, BF)
    eye_h = jnp.eye(MLA_HEADS, dtype=F32)
    lw['w_uv_bd'] = jnp.einsum('rhd,hg->hrgd', w['w_uv'][l], eye_h).reshape(
        MLA_HEADS * KV_RANK, MLA_HEADS * V_DIM).astype(BF)
    for name in ('w_br_a', 'w_br_b', 'w_br_c', 'w_out', 'ssm_w_glu'):
        lw[name] = w[name][l].astype(BF)
    lw['g_ffn'] = w['g_ffn'][l][None, :]

    lam_re, lam_im = w['ssm_lam_re'][l], w['ssm_lam_im'][l]
    dt = jnp.exp(w['ssm_log_dt'][l])[:, None]
    decay = jnp.exp(lam_re * dt)
    ab_re, ab_im = decay * jnp.cos(lam_im * dt), decay * jnp.sin(lam_im * dt)
    inv_den = 1.0 / (lam_re * lam_re + lam_im * lam_im)
    nr, ni = ab_re - 1.0, ab_im
    co_re = (nr * lam_re + ni * lam_im) * inv_den
    co_im = (ni * lam_re - nr * lam_im) * inv_den
    b_re, b_im = w['ssm_b_re'][l], w['ssm_b_im'][l]
    bb_re = co_re[..., None] * b_re - co_im[..., None] * b_im
    bb_im = co_re[..., None] * b_im + co_im[..., None] * b_re
    eye_g = jnp.eye(SSM_GROUPS, dtype=F32)
    bd = lambda bb: jnp.einsum('gnp,gh->gphn', bb, eye_g).reshape(SSM_WIDTH, SSM_FLAT)
    lw['ssm_bbd'] = jnp.concatenate([bd(bb_re), bd(bb_im)], axis=1).astype(BF)
    cd = lambda cc: jnp.einsum('gpn,gh->gnhp', cc, eye_g).reshape(SSM_FLAT, SSM_WIDTH)
    lw['ssm_cbd'] = jnp.concatenate([cd(w['ssm_c_re'][l]), -cd(w['ssm_c_im'][l])], axis=0).astype(BF)
    lw['ssm_a_re'] = ab_re.reshape(1, SSM_FLAT)
    lw['ssm_a_im'] = ab_im.reshape(1, SSM_FLAT)
    lw['ssm_d'] = w['ssm_d'][l][None, :]
    lw['ssm_b_glu'] = w['ssm_b_glu'][l][None, :]
    return lw


def _rope_tables(q_start, t, tm):
    half = ROPE_DIM // 2
    inv = 1.0 / (ROPE_THETA ** (jnp.arange(0, ROPE_DIM, 2, dtype=F32) / ROPE_DIM))
    ang = (q_start + jnp.arange(t)).astype(F32)[:, None] * inv[None, :]
    cos, sin = jnp.cos(ang), jnp.sin(ang)
    z16 = jnp.zeros_like(cos)
    zrest = jnp.zeros((t, LANES - ROPE_DIM), F32)
    tabs = [jnp.tile(cos, (1, MLA_HEADS)), jnp.tile(sin, (1, MLA_HEADS)),
            jnp.concatenate([cos, cos, zrest], axis=1),
            jnp.concatenate([-sin, z16, zrest], axis=1),
            jnp.concatenate([z16, sin, zrest], axis=1)]
    if tm > t:
        tabs = [jnp.tile(a, (tm // t, 1)) for a in tabs]
    return tabs


def _trunk(x, q_start, caches, w, layers, ffn_w):
    bsz, t, d = x.shape
    n = bsz * t
    xf = x.reshape(n, d)
    tm = _pick_tile(n, 256)
    assert t % tm == 0 or tm % t == 0
    tabs = _rope_tables(q_start, t, tm)
    new_rows = []
    for l, lw in enumerate(layers):
        (qcat, kcat, lat, krope, u, qsb0, qsb1, ksb, vsb, gates) = _in_proj(xf, lw, tabs, tm)
        u_tm = u.reshape(bsz, t, SSM_WIDTH).transpose(1, 0, 2).reshape(n, SSM_WIDTH)
        if caches is None:
            olat = _mla_prompt(qcat, kcat, bsz, t)
            oc0, oc1 = _sb_prompt(qsb0, qsb1, ksb, vsb, bsz, t)
            h0_re = jnp.zeros((bsz, SSM_FLAT), F32)
            h0_im = h0_re
        else:
            lat_pool, kr_pool, k_pool, v_pool, s_re, s_im, page_table = caches
            olat = _mla_sample(qcat, kcat, lat_pool, kr_pool, page_table, l, bsz, t)
            oc0, oc1 = _sb_sample(qsb0, qsb1, ksb, vsb, k_pool, v_pool, page_table, l, bsz, t)
            h0_re = s_re[l].reshape(bsz, SSM_FLAT)
            h0_im = s_im[l].reshape(bsz, SSM_FLAT)
        ob_tm, h_re, h_im = _s5(u_tm, h0_re, h0_im, lw, bsz, t)
        ob = ob_tm.reshape(t, bsz, SSM_WIDTH).transpose(1, 0, 2).reshape(n, SSM_WIDTH)
        i = l // 2
        tm_f = _pick_tile(n, 1024)
        if l % 2 == 0:
            x1, hf = _merge(xf, olat, ob, oc0, oc1, gates, lw, None, tm)
            wg, wu, wd = ffn_w['dense'][i]
            xf = _ffn(hf, x1, wg, wu, wd, tm_f, 256)
        else:
            x1, hf, gate = _merge(xf, olat, ob, oc0, oc1, gates, lw, ffn_w['router'][i], tm)
            wg, wu, wd = ffn_w['moe'][i]
            xf = _moe(hf, x1, gate, wg, wu, wd, tm_f, 256)
        new_rows.append((lat.reshape(bsz, t, KV_RANK), krope.reshape(bsz, t, ROPE_DIM),
                         ksb.reshape(bsz, t, SB_KV_HEADS, SB_HEAD_DIM),
                         vsb.reshape(bsz, t, SB_KV_HEADS, SB_HEAD_DIM),
                         h_re.reshape(bsz, SSM_GROUPS, SSM_STATE),
                         h_im.reshape(bsz, SSM_GROUPS, SSM_STATE)))
    y = _final_norm(xf, w['g_final'][None, :], tm).reshape(bsz, t, d)
    stacked = [jnp.stack([r[j] for r in new_rows]) for j in range(6)]
    return y, stacked


def kernel(x_prompt, x_sample, cache_mla_latent, cache_mla_krope, cache_sb_k, cache_sb_v, state_ssm_re, state_ssm_im, page_table, g_mix, w_in, g_q, g_kv, w_uq, w_uk, w_uv, ssm_lam_re, ssm_lam_im, ssm_log_dt, ssm_b_re, ssm_b_im, ssm_c_re, ssm_c_im, ssm_d, ssm_w_glu, ssm_b_glu, w_br_a, w_br_b, w_br_c, w_out, g_ffn, ffn_w_gate, ffn_w_up, ffn_w_down, moe_router, moe_w_gate, moe_w_up, moe_w_down, g_final):
    w = dict(g_mix=g_mix, w_in=w_in, g_q=g_q, g_kv=g_kv, w_uq=w_uq, w_uk=w_uk, w_uv=w_uv,
             ssm_lam_re=ssm_lam_re, ssm_lam_im=ssm_lam_im, ssm_log_dt=ssm_log_dt,
             ssm_b_re=ssm_b_re, ssm_b_im=ssm_b_im, ssm_c_re=ssm_c_re, ssm_c_im=ssm_c_im,
             ssm_d=ssm_d, ssm_w_glu=ssm_w_glu, ssm_b_glu=ssm_b_glu, w_br_a=w_br_a,
             w_br_b=w_br_b, w_br_c=w_br_c, w_out=w_out, g_ffn=g_ffn, g_final=g_final)
    depth = w_in.shape[0]
    layers = [_prep_layer(w, l) for l in range(depth)]
    d = w_in.shape[1]
    router = jnp.pad(moe_router, ((0, 0), (0, 0), (0, LANES - N_EXPERTS)))
    ffn_w = {
        'dense': [(ffn_w_gate[i].astype(BF), ffn_w_up[i].astype(BF), ffn_w_down[i].astype(BF))
                  for i in range(ffn_w_gate.shape[0])],
        'moe': [(moe_w_gate[i].astype(BF), moe_w_up[i].astype(BF), moe_w_down[i].astype(BF))
                for i in range(moe_w_gate.shape[0])],
        'router': [router[i] for i in range(router.shape[0])],
    }
    past_len = page_table.shape[1] * PAGE_KEYS
    y_p, (lat_p, kr_p, k_p, v_p, re_p, im_p) = _trunk(x_prompt, 0, None, w, layers, ffn_w)
    caches = (cache_mla_latent, cache_mla_krope, cache_sb_k, cache_sb_v,
              state_ssm_re, state_ssm_im, page_table)
    y_s, (lat_s, kr_s, k_s, v_s, re_s, im_s) = _trunk(x_sample, past_len, caches, w, layers, ffn_w)
    return (y_p, y_s, lat_p, lat_s, kr_p, kr_s, k_p, k_s, v_p, v_s, re_p, re_s, im_p, im_s)
```

```python
import functools
import math

import numpy as np
import jax
import jax.numpy as jnp
from jax import lax
from jax.experimental import pallas as pl
from jax.experimental.pallas import tpu as pltpu

F32 = jnp.float32
BF = jnp.bfloat16

MLA_HEADS = 8
QK_NOPE = 64
ROPE_DIM = 32
V_DIM = 64
Q_RANK = 256
KV_RANK = 128
ROPE_THETA = 10000.0
MLA_SCALE = (QK_NOPE + ROPE_DIM) ** -0.5
SSM_WIDTH = 512
SSM_GROUP_SIZE = 16
SSM_GROUPS = SSM_WIDTH // SSM_GROUP_SIZE
SSM_STATE = 64
SSM_FLAT = SSM_GROUPS * SSM_STATE
SB_HEADS = 8
SB_KV_HEADS = 2
SB_HEAD_DIM = 64
SB_GROUP = SB_HEADS // SB_KV_HEADS
SB_SCALE = SB_HEAD_DIM ** -0.5
N_EXPERTS = 8
EPS = 1e-6
NEG_BIG = -1e30
LOG2E = math.log2(math.e)
MLA_Q_SCALE = MLA_SCALE * LOG2E
SB_Q_SCALE = SB_SCALE * LOG2E

LANES = 128
SUBLANES = 8
VMEM_LIMIT_BYTES = 56 * 1024 * 1024

ROPE_PAD = LANES - ROPE_DIM
SEG_CQ = (0, 256)
SEG_CKV = (256, 384)
SEG_KROPE = (384, 512)
SEG_U = (512, 1024)
SEG_QSB = (1024, 1536)
SEG_KSB = (1536, 1664)
SEG_VSB = (1664, 1792)
SEG_GATES = (1792, 4864)
IN_PADDED = 4864

PAGE_KEYS = 128
SAMPLE_PAGES_PER_STEP = 32
SB_KEY_BLOCK = 256
MLA_KEY_BLOCK = 256
MLA_Q_POS_BLOCK = 256
MLA_ROW_SUB = 128
MLA_SCORE_LOOKAHEAD = 4
Q_POS_BLOCK = 128


def _params(*sem):
    return pltpu.CompilerParams(dimension_semantics=sem, vmem_limit_bytes=VMEM_LIMIT_BYTES)


def _pick_tile(n, pref):
    t = min(n, pref)
    while n % t:
        t -= SUBLANES
    return t


def _rms(x, g):
    return x * lax.rsqrt(jnp.mean(x * x, axis=-1, keepdims=True) + EPS) * g


def _dot(a, b):
    return jnp.dot(a, b, preferred_element_type=F32)


def _dot_nt(a, b):
    return lax.dot_general(a, b, (((1,), (1,)), ((), ())), preferred_element_type=F32)


def _const_spec(shape):
    nd = len(shape)
    return pl.BlockSpec(shape, lambda *_: (0,) * nd)


def _in_proj_kernel(x_ref, gmix_ref, win_ref, gq_ref, gkv_ref, wuq_ref, wukbd_ref, perm_ref,
                    cos8_ref, sin8_ref, ka_ref, kb_ref, kc_ref,
                    qcat_ref, kcat_ref, lat_ref, kr_ref, u_ref, qsb0_ref, qsb1_ref,
                    ksb_ref, vsb_ref, gates_ref):
    h = _rms(x_ref[...], gmix_ref[...]).astype(BF)

    def seg(bounds):
        return _dot(h, win_ref[:, bounds[0]:bounds[1]])

    qn = _rms(seg(SEG_CQ), gq_ref[...]).astype(BF)
    q = _dot(qn, wuq_ref[...])
    n_nope = MLA_HEADS * QK_NOPE
    x1 = q[:, n_nope:n_nope + LANES]
    x2 = q[:, n_nope + LANES:n_nope + 2 * LANES]
    c8 = cos8_ref[...]
    s8 = sin8_ref[...]
    rot = (jnp.concatenate([x1 * c8 - x2 * s8, x1 * s8 + x2 * c8], axis=1) * MLA_Q_SCALE).astype(BF)
    qr = _dot(rot, perm_ref[...])
    for pair in range(MLA_HEADS // 2):
        qn_pair = q[:, LANES * pair:LANES * (pair + 1)].astype(BF)
        ql = _dot(qn_pair, wukbd_ref[pair]) * MLA_Q_SCALE
        for j in range(2):
            hd = 2 * pair + j
            qcat_ref[:, 256 * hd:256 * hd + LANES] = ql[:, LANES * j:LANES * (j + 1)].astype(BF)
            qcat_ref[:, 256 * hd + LANES:256 * (hd + 1)] = qr[:, LANES * hd:LANES * (hd + 1)].astype(BF)

    lat = _rms(seg(SEG_CKV), gkv_ref[...])
    lat_ref[...] = lat
    zk = seg(SEG_KROPE)
    half = ROPE_DIM // 2
    kr = (zk * ka_ref[...] + pltpu.roll(zk, LANES - half, 1) * kb_ref[...]
          + pltpu.roll(zk, half, 1) * kc_ref[...])
    kr_ref[...] = kr[:, :ROPE_DIM]
    kcat_ref[:, :LANES] = lat.astype(BF)
    kcat_ref[:, LANES:] = kr.astype(BF)

    u_ref[...] = seg(SEG_U)
    qs = seg(SEG_QSB) * SB_Q_SCALE
    hw = SB_GROUP * SB_HEAD_DIM
    qsb0_ref[...] = qs[:, :hw].astype(BF)
    qsb1_ref[...] = qs[:, hw:].astype(BF)
    ksb_ref[...] = seg(SEG_KSB)
    vsb_ref[...] = seg(SEG_VSB)
    g0 = SEG_GATES[0]
    d = (SEG_GATES[1] - g0) // 3
    for j in range(3):
        gates_ref[:, d * j:d * (j + 1)] = jax.nn.sigmoid(seg((g0 + d * j, g0 + d * (j + 1)))).astype(BF)


def _in_proj(x, lw, tabs, tm):
    n, d = x.shape
    nblk = tabs[0].shape[0] // tm
    row = lambda w: pl.BlockSpec((tm, w), lambda i: (i, 0))
    tab = pl.BlockSpec((tm, LANES), lambda i: (i % nblk, 0))
    hw = SB_GROUP * SB_HEAD_DIM
    kvw = SB_KV_HEADS * SB_HEAD_DIM
    out_shape = (
        jax.ShapeDtypeStruct((n, MLA_HEADS * 256), BF),
        jax.ShapeDtypeStruct((n, 256), BF),
        jax.ShapeDtypeStruct((n, KV_RANK), F32),
        jax.ShapeDtypeStruct((n, ROPE_DIM), F32),
        jax.ShapeDtypeStruct((n, SSM_WIDTH), F32),
        jax.ShapeDtypeStruct((n, hw), BF),
        jax.ShapeDtypeStruct((n, hw), BF),
        jax.ShapeDtypeStruct((n, kvw), F32),
        jax.ShapeDtypeStruct((n, kvw), F32),
        jax.ShapeDtypeStruct((n, 3 * d), BF),
    )
    out_specs = tuple(row(s.shape[1]) for s in out_shape)
    consts = [lw['g_mix'], lw['w_in_p'], lw['g_q'], lw['g_kv'], lw['w_uq_p'], lw['w_uk_bd'], lw['perm']]
    return pl.pallas_call(
        _in_proj_kernel,
        grid=(n // tm,),
        in_specs=[row(d)] + [_const_spec(c.shape) for c in consts] + [tab] * 5,
        out_specs=out_specs,
        out_shape=out_shape,
        compiler_params=_params("parallel"),
        name="in_proj",
    )(x, *consts, *tabs)


def _mla_prompt_kernel(qi_ref, kj_ref, last_ref, q_ref, k_ref, o_ref, m_sc, acc_sc, *, tq, kb, rs):
    p = pl.program_id(1)
    qi = qi_ref[p]
    kj = kj_ref[p]
    rows = tq * MLA_HEADS

    @pl.when(kj == 0)
    def _():
        m_sc[...] = jnp.full(m_sc.shape, NEG_BIG, F32)
        acc_sc[...] = jnp.zeros(acc_sc.shape, F32)

    k = k_ref[...]
    v_ext = jnp.concatenate([k[:, :KV_RANK], jnp.ones((kb, LANES), BF)], axis=1)

    def process(masked):
        n_sub = rows // rs
        pending = []
        for step in range(n_sub + MLA_SCORE_LOOKAHEAD):
            if step < n_sub:
                pending.append(_dot_nt(q_ref[step * rs:(step + 1) * rs, :], k))
            r = step - MLA_SCORE_LOOKAHEAD
            if r < 0:
                continue
            sl = slice(r * rs, (r + 1) * rs)
            s = pending.pop(0)
            if masked:
                row = lax.broadcasted_iota(jnp.int32, s.shape, 0)
                col = lax.broadcasted_iota(jnp.int32, s.shape, 1)
                q_pos = qi * tq + (r * rs + row) // MLA_HEADS
                s = jnp.where(kj * kb + col <= q_pos, s, NEG_BIG)
            m_prev = m_sc[sl, :]
            m_new = jnp.maximum(m_prev, jnp.max(s, axis=1, keepdims=True))
            alpha = jnp.exp2(m_prev - m_new)
            pr = jnp.exp2(s - m_new).astype(BF)
            acc_sc[sl, :] = alpha * acc_sc[sl, :] + _dot(pr, v_ext)
            m_sc[sl, :] = m_new

    needs_mask = kj * kb + kb - 1 > qi * tq

    @pl.when(needs_mask)
    def _():
        process(True)

    @pl.when(jnp.logical_not(needs_mask))
    def _():
        process(False)

    @pl.when(last_ref[p] == 1)
    def _():
        acc = acc_sc[...]
        o_ref[...] = (acc[:, :KV_RANK] / acc[:, KV_RANK:KV_RANK + 1]).astype(BF)


def _causal_pairs(t, tq, kb, strict, descending):
    qi_l, kj_l, first_l, last_l = [], [], [], []
    for qi in range(t // tq):
        last_key = qi * tq + tq - 1 - (1 if strict else 0)
        ks = list(range(last_key // kb + 1))
        if descending:
            ks = ks[::-1]
        for n, kj in enumerate(ks):
            qi_l.append(qi)
            kj_l.append(kj)
            first_l.append(1 if n == 0 else 0)
            last_l.append(1 if n == len(ks) - 1 else 0)
    as_arr = lambda v: jnp.asarray(np.asarray(v, np.int32))
    return as_arr(qi_l), as_arr(kj_l), as_arr(first_l), as_arr(last_l)


def _mla_prompt(qcat, kcat, bsz, t):
    tq = min(MLA_Q_POS_BLOCK, t)
    kb = min(MLA_KEY_BLOCK, t)
    qi_t, kj_t, _, last_t = _causal_pairs(t, tq, kb, strict=False, descending=False)
    rows = tq * MLA_HEADS
    rs = min(MLA_ROW_SUB, rows)
    q2 = qcat.reshape(bsz * t * MLA_HEADS, 256)
    nq, nk = t // tq, t // kb
    grid_spec = pltpu.PrefetchScalarGridSpec(
        num_scalar_prefetch=3,
        grid=(bsz, int(qi_t.shape[0])),
        in_specs=[
            pl.BlockSpec((rows, 256), lambda b, p, qi, kj, la: (b * nq + qi[p], 0)),
            pl.BlockSpec((kb, 256), lambda b, p, qi, kj, la: (b * nk + kj[p], 0)),
        ],
        out_specs=pl.BlockSpec((rows, KV_RANK), lambda b, p, qi, kj, la: (b * nq + qi[p], 0)),
        scratch_shapes=[pltpu.VMEM((rows, 1), F32), pltpu.VMEM((rows, 2 * LANES), F32)],
    )
    o = pl.pallas_call(
        functools.partial(_mla_prompt_kernel, tq=tq, kb=kb, rs=rs),
        grid_spec=grid_spec,
        out_shape=jax.ShapeDtypeStruct((bsz * t * MLA_HEADS, KV_RANK), BF),
        compiler_params=_params("parallel", "arbitrary"),
        name="mla_prompt",
    )(qi_t, kj_t, last_t, q2, kcat)
    return o.reshape(bsz * t, MLA_HEADS * KV_RANK)


def _stream_page_chunks(pt_ref, pools, bufs, sems, chunk_of_step, layer, pb, page_axis, compute):
    b = pl.program_id(0)
    j = pl.program_id(1)
    nb = pl.num_programs(0)
    nch = pl.num_programs(1) - 1

    def copies(bb, jj, slot):
        chunk = chunk_of_step(jj)
        out = []
        for i in range(pb):
            page = pt_ref[bb, chunk * pb + i]
            for a, (pool, buf) in enumerate(zip(pools, bufs)):
                extent = pool.shape[2 + page_axis[a]]
                window = pl.ds(i * extent, extent)
                dst = buf.at[slot, window, :] if page_axis[a] == 0 else buf.at[slot, :, window]
                out.append(pltpu.make_async_copy(pool.at[layer, page], dst, sems.at[a, slot]))
        return out

    @pl.when((j == 0) & (b == 0))
    def _():
        for cp in copies(0, 1, 0):
            cp.start()

    @pl.when(j > 0)
    def _():
        slot = (b * nch + j - 1) % 2
        for cp in copies(b, j, slot):
            cp.wait()

        @pl.when(j < nch)
        def _():
            for cp in copies(b, j + 1, 1 - slot):
                cp.start()

        @pl.when((j == nch) & (b + 1 < nb))
        def _():
            for cp in copies(b + 1, 1, 1 - slot):
                cp.start()

        compute(slot)


def _mla_sample_kernel(pt_ref, q_ref, newk_ref, lat_hbm, kr_hbm, o_ref, lat_buf, kr_buf, sems,
                       m_sc, l_sc, acc_sc, *, pb, layer):
    j = pl.program_id(1)
    q = q_ref[...]

    def update(s, vals):
        m_prev = m_sc[...]
        m_new = jnp.maximum(m_prev, jnp.max(s, axis=1, keepdims=True))
        alpha = jnp.exp2(m_prev - m_new)
        pr = jnp.exp2(s - m_new)
        l_sc[...] = alpha * l_sc[...] + jnp.sum(pr, axis=1, keepdims=True)
        acc_sc[...] = alpha * acc_sc[...] + _dot(pr.astype(BF), vals)
        m_sc[...] = m_new

    @pl.when(j == 0)
    def _():
        m_sc[...] = jnp.full(m_sc.shape, NEG_BIG, F32)
        l_sc[...] = jnp.zeros(l_sc.shape, F32)
        acc_sc[...] = jnp.zeros(acc_sc.shape, F32)
        nk = newk_ref[...]
        s = _dot_nt(q, nk)
        row = lax.broadcasted_iota(jnp.int32, s.shape, 0)
        col = lax.broadcasted_iota(jnp.int32, s.shape, 1)
        s = jnp.where(col <= row // MLA_HEADS, s, NEG_BIG)
        update(s, nk[:, :KV_RANK])

    def paged(slot):
        q_lat = q[:, :KV_RANK]
        q_rope = q[:, KV_RANK:KV_RANK + ROPE_DIM]
        lat = lat_buf[slot].astype(BF)
        kr_t = kr_buf[slot].astype(BF)
        update(_dot_nt(q_lat, lat) + _dot(q_rope, kr_t), lat)

    _stream_page_chunks(pt_ref, (lat_hbm, kr_hbm), (lat_buf, kr_buf), sems, lambda jj: jj - 1,
                        layer, pb, (0, 1), paged)

    @pl.when(j == pl.num_programs(1) - 1)
    def _():
        o_ref[...] = (acc_sc[...] / l_sc[...]).astype(BF)


def _mla_sample(qcat, kcat, lat_pool, kr_pool, page_table, layer, bsz, t):
    n_pages = page_table.shape[1]
    pb = min(SAMPLE_PAGES_PER_STEP, n_pages)
    nch = n_pages // pb
    rows = t * MLA_HEADS
    q3 = qcat.reshape(bsz, rows, 256)
    newk = jnp.pad(kcat.reshape(bsz, t, 256), ((0, 0), (0, PAGE_KEYS - t), (0, 0)))
    kr_pool = jnp.swapaxes(kr_pool, 2, 3)
    hbm = pl.BlockSpec(memory_space=pl.ANY)
    grid_spec = pltpu.PrefetchScalarGridSpec(
        num_scalar_prefetch=1,
        grid=(bsz, nch + 1),
        in_specs=[pl.BlockSpec((None, rows, 256), lambda b, j, pt: (b, 0, 0)),
                  pl.BlockSpec((None, PAGE_KEYS, 256), lambda b, j, pt: (b, 0, 0)),
                  hbm, hbm],
        out_specs=pl.BlockSpec((None, rows, KV_RANK), lambda b, j, pt: (b, 0, 0)),
        scratch_shapes=[pltpu.VMEM((2, pb * PAGE_KEYS, KV_RANK), F32),
                        pltpu.VMEM((2, ROPE_DIM, pb * PAGE_KEYS), F32),
                        pltpu.SemaphoreType.DMA((2, 2)),
                        pltpu.VMEM((rows, 1), F32), pltpu.VMEM((rows, 1), F32),
                        pltpu.VMEM((rows, KV_RANK), F32)],
    )
    o = pl.pallas_call(
        functools.partial(_mla_sample_kernel, pb=pb, layer=layer),
        grid_spec=grid_spec,
        out_shape=jax.ShapeDtypeStruct((bsz, rows, KV_RANK), BF),
        compiler_params=_params("arbitrary", "arbitrary"),
        name="mla_sample",
    )(page_table, q3, newk, lat_pool, kr_pool)
    return o.reshape(bsz * t, MLA_HEADS * KV_RANK)


def _sb_log2_1m(z, mask):
    log_1m = -(jnp.maximum(z, 0.0) + jnp.log2(1.0 + jnp.exp2(-jnp.abs(z))))
    if mask is not None:
        log_1m = jnp.where(mask, log_1m, 0.0)
    return log_1m


def _suffix_sums(log_1m, tri):
    hi = log_1m.astype(BF)
    lo = (log_1m - hi.astype(F32)).astype(BF)
    return _dot(hi, tri) + _dot(lo, tri)


def _sb_weights(z, log_1m, between, mask):
    a = jnp.exp2(z + log_1m + between)
    if mask is not None:
        a = jnp.where(mask, a, 0.0)
    return a.astype(BF)


def _sb_prompt_kernel(qi_ref, kj_ref, first_ref, last_ref, q0_ref, q1_ref, k_ref, v_ref, tri_ref,
                      o0_ref, o1_ref, carry_sc, acc_sc, *, tq, kb):
    p = pl.program_id(1)
    qi = qi_ref[p]
    kj = kj_ref[p]

    @pl.when(first_ref[p] == 1)
    def _():
        carry_sc[...] = jnp.zeros(carry_sc.shape, F32)
        acc_sc[...] = jnp.zeros(acc_sc.shape, F32)

    rows = tq * SB_GROUP

    def process(masked):
        mask = None
        if masked:
            row = lax.broadcasted_iota(jnp.int32, (rows, kb), 0)
            col = lax.broadcasted_iota(jnp.int32, (rows, kb), 1)
            mask = (kj * kb + col) < (qi * tq + row // SB_GROUP)
        tri = tri_ref[...]
        kf = k_ref[...]
        vf = v_ref[...]
        for kv, q_ref in enumerate((q0_ref, q1_ref)):
            k = kf[:, SB_HEAD_DIM * kv:SB_HEAD_DIM * (kv + 1)].astype(BF)
            v = vf[:, SB_HEAD_DIM * kv:SB_HEAD_DIM * (kv + 1)].astype(BF)
            z = _dot_nt(q_ref[...], k)
            log_1m = _sb_log2_1m(z, mask)
            w = _suffix_sums(log_1m, tri)
            carry = carry_sc[kv]
            a = _sb_weights(z, log_1m, carry + w, mask)
            acc_sc[kv] = acc_sc[kv] + _dot(a, v)
            carry_sc[kv] = carry + (w + log_1m)[:, 0:1]

    needs_mask = kj * kb + kb - 1 >= qi * tq

    @pl.when(needs_mask)
    def _():
        process(True)

    @pl.when(jnp.logical_not(needs_mask))
    def _():
        process(False)

    @pl.when(last_ref[p] == 1)
    def _():
        o0_ref[...] = acc_sc[0].astype(BF)
        o1_ref[...] = acc_sc[1].astype(BF)


def _strict_lower_ones(n):
    j = np.arange(n)[:, None]
    s = np.arange(n)[None, :]
    return jnp.asarray((j > s).astype(np.float32), BF)


def _sb_prompt(qsb0, qsb1, ksb, vsb, bsz, t):
    tq = min(Q_POS_BLOCK, t)
    kb = min(SB_KEY_BLOCK, t)
    qi_t, kj_t, first_t, last_t = _causal_pairs(t, tq, kb, strict=True, descending=True)
    rows = tq * SB_GROUP
    nq, nk = t // tq, t // kb
    n = bsz * t
    q0 = qsb0.reshape(n * SB_GROUP, SB_HEAD_DIM)
    q1 = qsb1.reshape(n * SB_GROUP, SB_HEAD_DIM)
    kvw = SB_KV_HEADS * SB_HEAD_DIM
    qspec = pl.BlockSpec((rows, SB_HEAD_DIM), lambda b, p, qi, kj, fi, la: (b * nq + qi[p], 0))
    kspec = pl.BlockSpec((kb, kvw), lambda b, p, qi, kj, fi, la: (b * nk + kj[p], 0))
    grid_spec = pltpu.PrefetchScalarGridSpec(
        num_scalar_prefetch=4,
        grid=(bsz, int(qi_t.shape[0])),
        in_specs=[qspec, qspec, kspec, kspec,
                  pl.BlockSpec((kb, kb), lambda b, p, qi, kj, fi, la: (0, 0))],
        out_specs=(qspec, qspec),
        scratch_shapes=[pltpu.VMEM((SB_KV_HEADS, rows, 1), F32),
                        pltpu.VMEM((SB_KV_HEADS, rows, SB_HEAD_DIM), F32)],
    )
    o_sds = jax.ShapeDtypeStruct((n * SB_GROUP, SB_HEAD_DIM), BF)
    o0, o1 = pl.pallas_call(
        functools.partial(_sb_prompt_kernel, tq=tq, kb=kb),
        grid_spec=grid_spec,
        out_shape=(o_sds, o_sds),
        compiler_params=_params("parallel", "arbitrary"),
        name="sb_prompt",
    )(qi_t, kj_t, first_t, last_t, q0, q1, ksb, vsb, _strict_lower_ones(kb))
    hw = SB_GROUP * SB_HEAD_DIM
    return o0.reshape(n, hw), o1.reshape(n, hw)


def _sb_sample_kernel(pt_ref, q_ref, newk_ref, newv_ref, tri_new_ref, tri_ref, k_hbm, v_hbm,
                      o_ref, k_buf, v_buf, sems, carry_sc, acc_sc, *, pb, t_new, layer):
    j = pl.program_id(1)
    q = q_ref[...]
    rows = q.shape[0]
    rows_per_kv = t_new * SB_GROUP

    @pl.when(j == 0)
    def _():
        nk_t = newk_ref[...].astype(BF)
        nv_t = newv_ref[...].astype(BF)
        row = lax.broadcasted_iota(jnp.int32, (rows, PAGE_KEYS), 0)
        col = lax.broadcasted_iota(jnp.int32, (rows, PAGE_KEYS), 1)
        mask = col < (row % rows_per_kv) // SB_GROUP
        z = _dot(q, nk_t)
        log_1m = _sb_log2_1m(z, mask)
        w = _suffix_sums(log_1m, tri_new_ref[...])
        a = _sb_weights(z, log_1m, w, mask)
        acc_sc[...] = _dot_nt(a, nv_t)
        carry_sc[...] = (w + log_1m)[:, 0:1]

    def paged(slot):
        k_t = k_buf[slot].astype(BF)
        v_t = v_buf[slot].astype(BF)
        z = _dot(q, k_t)
        log_1m = _sb_log2_1m(z, None)
        sub = tri_ref.shape[0]
        nsub = z.shape[1] // sub
        stacked = jnp.concatenate([log_1m[:, sub * s:sub * (s + 1)] for s in range(nsub)], axis=0)
        w = _suffix_sums(stacked, tri_ref[...])
        totals = (w + stacked)[:, 0:1]
        carry = carry_sc[...]
        between = [None] * nsub
        for s in reversed(range(nsub)):
            between[s] = carry + w[rows * s:rows * (s + 1), :]
            carry = carry + totals[rows * s:rows * (s + 1), :]
        a = _sb_weights(z, log_1m, jnp.concatenate(between, axis=1), None)
        acc_sc[...] = acc_sc[...] + _dot_nt(a, v_t)
        carry_sc[...] = carry

    _stream_page_chunks(pt_ref, (k_hbm, v_hbm), (k_buf, v_buf), sems,
                        lambda jj: pl.num_programs(1) - 1 - jj, layer, pb, (1, 1), paged)

    @pl.when(j == pl.num_programs(1) - 1)
    def _():
        o_ref[...] = acc_sc[...]


def _sb_sample(qsb0, qsb1, ksb, vsb, k_pool, v_pool, page_table, layer, bsz, t):
    n_pages = page_table.shape[1]
    pb = min(SAMPLE_PAGES_PER_STEP, n_pages)
    nch = n_pages // pb
    kvw = SB_KV_HEADS * SB_HEAD_DIM
    rows = SB_KV_HEADS * t * SB_GROUP
    sub = min(SB_KEY_BLOCK, pb * PAGE_KEYS)
    q0 = qsb0.reshape(bsz, t * SB_GROUP, SB_HEAD_DIM)
    q1 = qsb1.reshape(bsz, t * SB_GROUP, SB_HEAD_DIM)
    zq = jnp.zeros_like(q0)
    q = jnp.concatenate([jnp.concatenate([q0, zq], axis=2), jnp.concatenate([zq, q1], axis=2)], axis=1)
    pad = ((0, 0), (0, 0), (0, PAGE_KEYS - t))
    newk = jnp.pad(jnp.swapaxes(ksb.reshape(bsz, t, kvw), 1, 2), pad)
    newv = jnp.pad(jnp.swapaxes(vsb.reshape(bsz, t, kvw), 1, 2), pad)
    depth, n_phys = k_pool.shape[0], k_pool.shape[1]
    kp = jnp.transpose(k_pool, (0, 1, 3, 4, 2)).reshape(depth, n_phys, kvw, PAGE_KEYS)
    vp = jnp.transpose(v_pool, (0, 1, 3, 4, 2)).reshape(depth, n_phys, kvw, PAGE_KEYS)

    per_b = lambda r, w: pl.BlockSpec((None, r, w), lambda b, j, pt: (b, 0, 0))
    const2 = lambda r: pl.BlockSpec((r, r), lambda b, j, pt: (0, 0))
    hbm = pl.BlockSpec(memory_space=pl.ANY)
    grid_spec = pltpu.PrefetchScalarGridSpec(
        num_scalar_prefetch=1,
        grid=(bsz, nch + 1),
        in_specs=[per_b(rows, kvw), per_b(kvw, PAGE_KEYS), per_b(kvw, PAGE_KEYS),
                  const2(PAGE_KEYS), const2(sub), hbm, hbm],
        out_specs=per_b(rows, kvw),
        scratch_shapes=[pltpu.VMEM((2, kvw, pb * PAGE_KEYS), F32),
                        pltpu.VMEM((2, kvw, pb * PAGE_KEYS), F32),
                        pltpu.SemaphoreType.DMA((2, 2)),
                        pltpu.VMEM((rows, 1), F32), pltpu.VMEM((rows, kvw), F32)],
    )
    o = pl.pallas_call(
        functools.partial(_sb_sample_kernel, pb=pb, t_new=t, layer=layer),
        grid_spec=grid_spec,
        out_shape=jax.ShapeDtypeStruct((bsz, rows, kvw), F32),
        compiler_params=_params("arbitrary", "arbitrary"),
        name="sb_sample",
    )(page_table, q, newk, newv, _strict_lower_ones(PAGE_KEYS), _strict_lower_ones(sub), kp, vp)
    o = o.reshape(bsz, SB_KV_HEADS, t * SB_GROUP, SB_KV_HEADS, SB_HEAD_DIM)
    hw = SB_GROUP * SB_HEAD_DIM
    o0 = o[:, 0, :, 0, :].reshape(bsz * t, hw).astype(BF)
    o1 = o[:, 1, :, 1, :].reshape(bsz * t, hw).astype(BF)
    return o0, o1


def _s5_kernel(u_ref, h0re_ref, h0im_ref, are_ref, aim_ref, bbd_ref, cbd_ref, dskip_ref, wglu_ref,
               bglu_ref, ob_ref, hre_ref, him_ref, s_sc, st_sc, *, bn, tc, lc):
    c = pl.program_id(0)

    @pl.when(c == 0)
    def _():
        st_sc[0] = h0re_ref[...]
        st_sc[1] = h0im_ref[...]

    u = u_ref[...]
    ub = u.astype(BF)
    nflat = SSM_FLAT
    for jb in range(2 * nflat // 512):
        s_sc[:, 512 * jb:512 * (jb + 1)] = _dot(ub, bbd_ref[:, 512 * jb:512 * (jb + 1)])

    for ci in range(nflat // lc):
        lo = ci * lc
        ar = jnp.broadcast_to(are_ref[:, lo:lo + lc], (bn, lc))
        ai = jnp.broadcast_to(aim_ref[:, lo:lo + lc], (bn, lc))

        def body(t, carry, lo=lo, ar=ar, ai=ai):
            hr, hi = carry
            r0 = pl.multiple_of(t * bn, bn)
            br = s_sc[pl.ds(r0, bn), lo:lo + lc]
            bi = s_sc[pl.ds(r0, bn), nflat + lo:nflat + lo + lc]
            nr = ar * hr - ai * hi + br
            ni = ar * hi + ai * hr + bi
            s_sc[pl.ds(r0, bn), lo:lo + lc] = nr
            s_sc[pl.ds(r0, bn), nflat + lo:nflat + lo + lc] = ni
            return nr, ni

        hr, hi = lax.fori_loop(0, tc, body, (st_sc[0, :, lo:lo + lc], st_sc[1, :, lo:lo + lc]))
        st_sc[0, :, lo:lo + lc] = hr
        st_sc[1, :, lo:lo + lc] = hi

    y = dskip_ref[...] * u
    for jb in range(2 * nflat // 1024):
        y = y + _dot(s_sc[:, 1024 * jb:1024 * (jb + 1)].astype(BF), cbd_ref[1024 * jb:1024 * (jb + 1), :])
    g = jax.nn.gelu(y)
    gate = jax.nn.sigmoid(_dot(g.astype(BF), wglu_ref[...]) + bglu_ref[...])
    ob_ref[...] = (g * gate).astype(BF)

    @pl.when(c == pl.num_programs(0) - 1)
    def _():
        hre_ref[...] = st_sc[0]
        him_ref[...] = st_sc[1]


def _s5(u_tm, h0_re, h0_im, lw, bn, t):
    tc = t
    while tc * bn > 512 and tc % 2 == 0:
        tc //= 2
    rows = tc * bn
    lc = 256 if bn <= 16 else LANES
    nflat = SSM_FLAT
    consts = [lw['ssm_a_re'], lw['ssm_a_im'], lw['ssm_bbd'], lw['ssm_cbd'], lw['ssm_d'],
              lw['ssm_w_glu'], lw['ssm_b_glu']]
    st_spec = pl.BlockSpec((bn, nflat), lambda c: (0, 0))
    st_sds = jax.ShapeDtypeStruct((bn, nflat), F32)
    return pl.pallas_call(
        functools.partial(_s5_kernel, bn=bn, tc=tc, lc=lc),
        grid=(t // tc,),
        in_specs=[pl.BlockSpec((rows, SSM_WIDTH), lambda c: (c, 0)), st_spec, st_spec]
                 + [_const_spec(a.shape) for a in consts],
        out_specs=(pl.BlockSpec((rows, SSM_WIDTH), lambda c: (c, 0)), st_spec, st_spec),
        out_shape=(jax.ShapeDtypeStruct((t * bn, SSM_WIDTH), BF), st_sds, st_sds),
        scratch_shapes=[pltpu.VMEM((rows, 2 * nflat), F32), pltpu.VMEM((2, bn, nflat), F32)],
        compiler_params=_params("arbitrary"),
        name="s5",
    )(u_tm, h0_re, h0_im, *consts)


def _merge_kernel(x_ref, olat_ref, ob_ref, oc0_ref, oc1_ref, gates_ref, wuv_ref, wa_ref, wb_ref,
                  wc_ref, wout_ref, gffn_ref, *rest, routed):
    if routed:
        router_ref, x1_ref, hf_ref, gate_ref = rest
    else:
        x1_ref, hf_ref = rest
    d = x_ref.shape[1]
    hw = SB_GROUP * SB_HEAD_DIM
    o_a = _dot(olat_ref[...], wuv_ref[...]).astype(BF)
    br_a = _dot(o_a, wa_ref[...])
    br_b = _dot(ob_ref[...], wb_ref[...])
    br_c = _dot(oc0_ref[...], wc_ref[:hw, :]) + _dot(oc1_ref[...], wc_ref[hw:, :])
    merged = (gates_ref[:, :d].astype(F32) * br_a + gates_ref[:, d:2 * d].astype(F32) * br_b
              + gates_ref[:, 2 * d:].astype(F32) * br_c)
    x1 = x_ref[...] + _dot(merged.astype(BF), wout_ref[...])
    x1_ref[...] = x1
    hf = _rms(x1, gffn_ref[...])
    hf_ref[...] = hf.astype(BF)
    if routed:
        logits = jnp.dot(hf, router_ref[...], preferred_element_type=F32, precision=lax.Precision.HIGHEST)
        lane = lax.broadcasted_iota(jnp.int32, logits.shape, 1)
        lg = jnp.where(lane < N_EXPERTS, logits, NEG_BIG)
        v1 = jnp.max(lg, axis=1, keepdims=True)
        i1 = jnp.min(jnp.where(lg == v1, lane, LANES), axis=1, keepdims=True)
        lg2 = jnp.where(lane == i1, NEG_BIG, lg)
        v2 = jnp.max(lg2, axis=1, keepdims=True)
        i2 = jnp.min(jnp.where(lg2 == v2, lane, LANES), axis=1, keepdims=True)
        e2 = jnp.exp(v2 - v1)
        p1 = 1.0 / (1.0 + e2)
        p2 = e2 / (1.0 + e2)
        gate_ref[...] = jnp.where(lane == i1, p1, 0.0) + jnp.where(lane == i2, p2, 0.0)


def _merge(x, olat, ob, oc0, oc1, gates, lw, router, tm):
    n, d = x.shape
    routed = router is not None
    row = lambda w: pl.BlockSpec((tm, w), lambda i: (i, 0))
    consts = [lw['w_uv_bd'], lw['w_br_a'], lw['w_br_b'], lw['w_br_c'], lw['w_out'], lw['g_ffn']]
    if routed:
        consts.append(router)
    acts = [x, olat, ob, oc0, oc1, gates]
    out_shape = [jax.ShapeDtypeStruct((n, d), F32), jax.ShapeDtypeStruct((n, d), BF)]
    if routed:
        out_shape.append(jax.ShapeDtypeStruct((n, LANES), F32))
    return pl.pallas_call(
        functools.partial(_merge_kernel, routed=routed),
        grid=(n // tm,),
        in_specs=[row(a.shape[1]) for a in acts] + [_const_spec(c.shape) for c in consts],
        out_specs=tuple(row(s.shape[1]) for s in out_shape),
        out_shape=tuple(out_shape),
        compiler_params=_params("parallel"),
        name="merge",
    )(*acts, *consts)


def _ffn_kernel(h_ref, x_ref, wg_ref, wu_ref, wd_ref, o_ref, acc_sc):
    f = pl.program_id(1)

    @pl.when(f == 0)
    def _():
        acc_sc[...] = jnp.zeros(acc_sc.shape, F32)

    h = h_ref[...]
    act = (jax.nn.silu(_dot(h, wg_ref[...])) * _dot(h, wu_ref[...])).astype(BF)
    acc_sc[...] += _dot(act, wd_ref[...])

    @pl.when(f == pl.num_programs(1) - 1)
    def _():
        o_ref[...] = x_ref[...] + acc_sc[...]


def _ffn(hf, x, wg, wu, wd, tm, tf):
    n, d = x.shape
    dff = wg.shape[1]
    return pl.pallas_call(
        _ffn_kernel,
        grid=(n // tm, dff // tf),
        in_specs=[pl.BlockSpec((tm, d), lambda i, f: (i, 0)),
                  pl.BlockSpec((tm, d), lambda i, f: (i, 0)),
                  pl.BlockSpec((d, tf), lambda i, f: (0, f)),
                  pl.BlockSpec((d, tf), lambda i, f: (0, f)),
                  pl.BlockSpec((tf, d), lambda i, f: (f, 0))],
        out_specs=pl.BlockSpec((tm, d), lambda i, f: (i, 0)),
        out_shape=jax.ShapeDtypeStruct((n, d), F32),
        scratch_shapes=[pltpu.VMEM((tm, d), F32)],
        compiler_params=_params("parallel", "arbitrary"),
        name="ffn_dense",
    )(hf, x, wg, wu, wd)


def _moe_kernel(h_ref, x_ref, gate_ref, wg_ref, wu_ref, wd_ref, o_ref, acc_sc):
    e = pl.program_id(1)
    f = pl.program_id(2)

    @pl.when((e == 0) & (f == 0))
    def _():
        acc_sc[...] = jnp.zeros(acc_sc.shape, F32)

    h = h_ref[...]
    act = (jax.nn.silu(_dot(h, wg_ref[...])) * _dot(h, wu_ref[...])).astype(BF)
    gate = gate_ref[...]
    lane = lax.broadcasted_iota(jnp.int32, gate.shape, 1)
    g_e = jnp.sum(jnp.where(lane == e, gate, 0.0), axis=1, keepdims=True)
    acc_sc[...] += g_e * _dot(act, wd_ref[...])

    @pl.when((e == pl.num_programs(1) - 1) & (f == pl.num_programs(2) - 1))
    def _():
        o_ref[...] = x_ref[...] + acc_sc[...]


def _moe(hf, x, gate, wg, wu, wd, tm, tf):
    n, d = x.shape
    ne, _, dff = wg.shape
    return pl.pallas_call(
        _moe_kernel,
        grid=(n // tm, ne, dff // tf),
        in_specs=[pl.BlockSpec((tm, d), lambda i, e, f: (i, 0)),
                  pl.BlockSpec((tm, d), lambda i, e, f: (i, 0)),
                  pl.BlockSpec((tm, LANES), lambda i, e, f: (i, 0)),
                  pl.BlockSpec((None, d, tf), lambda i, e, f: (e, 0, f)),
                  pl.BlockSpec((None, d, tf), lambda i, e, f: (e, 0, f)),
                  pl.BlockSpec((None, tf, d), lambda i, e, f: (e, f, 0))],
        out_specs=pl.BlockSpec((tm, d), lambda i, e, f: (i, 0)),
        out_shape=jax.ShapeDtypeStruct((n, d), F32),
        scratch_shapes=[pltpu.VMEM((tm, d), F32)],
        compiler_params=_params("parallel", "arbitrary", "arbitrary"),
        name="moe",
    )(hf, x, gate, wg, wu, wd)


def _final_norm_kernel(x_ref, g_ref, o_ref):
    o_ref[...] = _rms(x_ref[...], g_ref[...])


def _final_norm(x, g, tm):
    n, d = x.shape
    return pl.pallas_call(
        _final_norm_kernel,
        grid=(n // tm,),
        in_specs=[pl.BlockSpec((tm, d), lambda i: (i, 0)), _const_spec(g.shape)],
        out_specs=pl.BlockSpec((tm, d), lambda i: (i, 0)),
        out_shape=jax.ShapeDtypeStruct((n, d), F32),
        compiler_params=_params("parallel"),
        name="final_norm",
    )(x, g)


def _prep_layer(w, l):
    d = w['w_in'].shape[1]
    lw = {}
    lw['g_mix'] = w['g_mix'][l][None, :]
    w_in = w['w_in'][l]
    split = SEG_KROPE[0] + ROPE_DIM
    lw['w_in_p'] = jnp.concatenate(
        [w_in[:, :split], jnp.zeros((d, ROPE_PAD), w_in.dtype), w_in[:, split:]], axis=1).astype(BF)
    lw['g_q'] = w['g_q'][l][None, :]
    lw['g_kv'] = w['g_kv'][l][None, :]
    half = ROPE_DIM // 2
    w3 = w['w_uq'][l].reshape(Q_RANK, MLA_HEADS, QK_NOPE + ROPE_DIM)
    lw['w_uq_p'] = jnp.concatenate(
        [w3[:, :, :QK_NOPE].reshape(Q_RANK, -1),
         w3[:, :, QK_NOPE:QK_NOPE + half].reshape(Q_RANK, -1),
         w3[:, :, QK_NOPE + half:].reshape(Q_RANK, -1)], axis=1).astype(BF)
    wk = jnp.transpose(w['w_uk'][l], (1, 2, 0))
    zk = jnp.zeros((QK_NOPE, KV_RANK), wk.dtype)
    lw['w_uk_bd'] = jnp.stack([
        jnp.concatenate([jnp.concatenate([wk[2 * p], zk], axis=1),
                         jnp.concatenate([zk, wk[2 * p + 1]], axis=1)], axis=0)
        for p in range(MLA_HEADS // 2)]).astype(BF)
    perm = np.zeros((2 * LANES, MLA_HEADS * LANES), np.float32)
    for h in range(MLA_HEADS):
        for i in range(half):
            perm[h * half + i, LANES * h + i] = 1.0
            perm[LANES + h * half + i, LANES * h + half + i] = 1.0
    lw['perm'] = jnp.asarray(perm, BF)
    eye_h = jnp.eye(MLA_HEADS, dtype=F32)
    lw['w_uv_bd'] = jnp.einsum('rhd,hg->hrgd', w['w_uv'][l], eye_h).reshape(
        MLA_HEADS * KV_RANK, MLA_HEADS * V_DIM).astype(BF)
    for name in ('w_br_a', 'w_br_b', 'w_br_c', 'w_out', 'ssm_w_glu'):
        lw[name] = w[name][l].astype(BF)
    lw['g_ffn'] = w['g_ffn'][l][None, :]

    lam_re, lam_im = w['ssm_lam_re'][l], w['ssm_lam_im'][l]
    dt = jnp.exp(w['ssm_log_dt'][l])[:, None]
    decay = jnp.exp(lam_re * dt)
    ab_re, ab_im = decay * jnp.cos(lam_im * dt), decay * jnp.sin(lam_im * dt)
    inv_den = 1.0 / (lam_re * lam_re + lam_im * lam_im)
    nr, ni = ab_re - 1.0, ab_im
    co_re = (nr * lam_re + ni * lam_im) * inv_den
    co_im = (ni * lam_re - nr * lam_im) * inv_den
    b_re, b_im = w['ssm_b_re'][l], w['ssm_b_im'][l]
    bb_re = co_re[..., None] * b_re - co_im[..., None] * b_im
    bb_im = co_re[..., None] * b_im + co_im[..., None] * b_re
    eye_g = jnp.eye(SSM_GROUPS, dtype=F32)
    bd = lambda bb: jnp.einsum('gnp,gh->gphn', bb, eye_g).reshape(SSM_WIDTH, SSM_FLAT)
    lw['ssm_bbd'] = jnp.concatenate([bd(bb_re), bd(bb_im)], axis=1).astype(BF)
    cd = lambda cc: jnp.einsum('gpn,gh->gnhp', cc, eye_g).reshape(SSM_FLAT, SSM_WIDTH)
    lw['ssm_cbd'] = jnp.concatenate([cd(w['ssm_c_re'][l]), -cd(w['ssm_c_im'][l])], axis=0).astype(BF)
    lw['ssm_a_re'] = ab_re.reshape(1, SSM_FLAT)
    lw['ssm_a_im'] = ab_im.reshape(1, SSM_FLAT)
    lw['ssm_d'] = w['ssm_d'][l][None, :]
    lw['ssm_b_glu'] = w['ssm_b_glu'][l][None, :]
    return lw


def _rope_tables(q_start, t, tm):
    half = ROPE_DIM // 2
    inv = 1.0 / (ROPE_THETA ** (jnp.arange(0, ROPE_DIM, 2, dtype=F32) / ROPE_DIM))
    ang = (q_start + jnp.arange(t)).astype(F32)[:, None] * inv[None, :]
    cos, sin = jnp.cos(ang), jnp.sin(ang)
    z16 = jnp.zeros_like(cos)
    zrest = jnp.zeros((t, LANES - ROPE_DIM), F32)
    tabs = [jnp.tile(cos, (1, MLA_HEADS)), jnp.tile(sin, (1, MLA_HEADS)),
            jnp.concatenate([cos, cos, zrest], axis=1),
            jnp.concatenate([-sin, z16, zrest], axis=1),
            jnp.concatenate([z16, sin, zrest], axis=1)]
    if tm > t:
        tabs = [jnp.tile(a, (tm // t, 1)) for a in tabs]
    return tabs


def _trunk(x, q_start, caches, w, layers, ffn_w):
    bsz, t, d = x.shape
    n = bsz * t
    xf = x.reshape(n, d)
    tm = _pick_tile(n, 256)
    assert t % tm == 0 or tm % t == 0
    tabs = _rope_tables(q_start, t, tm)
    new_rows = []
    for l, lw in enumerate(layers):
        (qcat, kcat, lat, krope, u, qsb0, qsb1, ksb, vsb, gates) = _in_proj(xf, lw, tabs, tm)
        u_tm = u.reshape(bsz, t, SSM_WIDTH).transpose(1, 0, 2).reshape(n, SSM_WIDTH)
        if caches is None:
            olat = _mla_prompt(qcat, kcat, bsz, t)
            oc0, oc1 = _sb_prompt(qsb0, qsb1, ksb, vsb, bsz, t)
            h0_re = jnp.zeros((bsz, SSM_FLAT), F32)
            h0_im = h0_re
        else:
            lat_pool, kr_pool, k_pool, v_pool, s_re, s_im, page_table = caches
            olat = _mla_sample(qcat, kcat, lat_pool, kr_pool, page_table, l, bsz, t)
            oc0, oc1 = _sb_sample(qsb0, qsb1, ksb, vsb, k_pool, v_pool, page_table, l, bsz, t)
            h0_re = s_re[l].reshape(bsz, SSM_FLAT)
            h0_im = s_im[l].reshape(bsz, SSM_FLAT)
        ob_tm, h_re, h_im = _s5(u_tm, h0_re, h0_im, lw, bsz, t)
        ob = ob_tm.reshape(t, bsz, SSM_WIDTH).transpose(1, 0, 2).reshape(n, SSM_WIDTH)
        i = l // 2
        tm_f = _pick_tile(n, 1024)
        if l % 2 == 0:
            x1, hf = _merge(xf, olat, ob, oc0, oc1, gates, lw, None, tm)
            wg, wu, wd = ffn_w['dense'][i]
            xf = _ffn(hf, x1, wg, wu, wd, tm_f, 256)
        else:
            x1, hf, gate = _merge(xf, olat, ob, oc0, oc1, gates, lw, ffn_w['router'][i], tm)
            wg, wu, wd = ffn_w['moe'][i]
            xf = _moe(hf, x1, gate, wg, wu, wd, tm_f, 512)
        new_rows.append((lat.reshape(bsz, t, KV_RANK), krope.reshape(bsz, t, ROPE_DIM),
                         ksb.reshape(bsz, t, SB_KV_HEADS, SB_HEAD_DIM),
                         vsb.reshape(bsz, t, SB_KV_HEADS, SB_HEAD_DIM),
                         h_re.reshape(bsz, SSM_GROUPS, SSM_STATE),
                         h_im.reshape(bsz, SSM_GROUPS, SSM_STATE)))
    y = _final_norm(xf, w['g_final'][None, :], tm).reshape(bsz, t, d)
    stacked = [jnp.stack([r[j] for r in new_rows]) for j in range(6)]
    return y, stacked


def kernel(x_prompt, x_sample, cache_mla_latent, cache_mla_krope, cache_sb_k, cache_sb_v, state_ssm_re, state_ssm_im, page_table, g_mix, w_in, g_q, g_kv, w_uq, w_uk, w_uv, ssm_lam_re, ssm_lam_im, ssm_log_dt, ssm_b_re, ssm_b_im, ssm_c_re, ssm_c_im, ssm_d, ssm_w_glu, ssm_b_glu, w_br_a, w_br_b, w_br_c, w_out, g_ffn, ffn_w_gate, ffn_w_up, ffn_w_down, moe_router, moe_w_gate, moe_w_up, moe_w_down, g_final):
    w = dict(g_mix=g_mix, w_in=w_in, g_q=g_q, g_kv=g_kv, w_uq=w_uq, w_uk=w_uk, w_uv=w_uv,
             ssm_lam_re=ssm_lam_re, ssm_lam_im=ssm_lam_im, ssm_log_dt=ssm_log_dt,
             ssm_b_re=ssm_b_re, ssm_b_im=ssm_b_im, ssm_c_re=ssm_c_re, ssm_c_im=ssm_c_im,
             ssm_d=ssm_d, ssm_w_glu=ssm_w_glu, ssm_b_glu=ssm_b_glu, w_br_a=w_br_a,
             w_br_b=w_br_b, w_br_c=w_br_c, w_out=w_out, g_ffn=g_ffn, g_final=g_final)
    depth = w_in.shape[0]
    layers = [_prep_layer(w, l) for l in range(depth)]
    d = w_in.shape[1]
    router = jnp.pad(moe_router, ((0, 0), (0, 0), (0, LANES - N_EXPERTS)))
    ffn_w = {
        'dense': [(ffn_w_gate[i].astype(BF), ffn_w_up[i].astype(BF), ffn_w_down[i].astype(BF))
                  for i in range(ffn_w_gate.shape[0])],
        'moe': [(moe_w_gate[i].astype(BF), moe_w_up[i].astype(BF), moe_w_down[i].astype(BF))
                for i in range(moe_w_gate.shape[0])],
        'router': [router[i] for i in range(router.shape[0])],
    }
    past_len = page_table.shape[1] * PAGE_KEYS
    y_p, (lat_p, kr_p, k_p, v_p, re_p, im_p) = _trunk(x_prompt, 0, None, w, layers, ffn_w)
    caches = (cache_mla_latent, cache_mla_krope, cache_sb_k, cache_sb_v,
              state_ssm_re, state_ssm_im, page_table)
    y_s, (lat_s, kr_s, k_s, v_s, re_s, im_s) = _trunk(x_sample, past_len, caches, w, layers, ffn_w)
    return (y_p, y_s, lat_p, lat_s, kr_p, kr_s, k_p, k_s, v_p, v_s, re_p, re_s, im_p, im_s)
```

```python
import functools
import math

import numpy as np
import jax
import jax.numpy as jnp
from jax import lax
from jax.experimental import pallas as pl
from jax.experimental.pallas import tpu as pltpu

F32 = jnp.float32
BF = jnp.bfloat16

MLA_HEADS = 8
QK_NOPE = 64
ROPE_DIM = 32
V_DIM = 64
Q_RANK = 256
KV_RANK = 128
ROPE_THETA = 10000.0
MLA_SCALE = (QK_NOPE + ROPE_DIM) ** -0.5
SSM_WIDTH = 512
SSM_GROUP_SIZE = 16
SSM_GROUPS = SSM_WIDTH // SSM_GROUP_SIZE
SSM_STATE = 64
SSM_FLAT = SSM_GROUPS * SSM_STATE
SB_HEADS = 8
SB_KV_HEADS = 2
SB_HEAD_DIM = 64
SB_GROUP = SB_HEADS // SB_KV_HEADS
SB_SCALE = SB_HEAD_DIM ** -0.5
N_EXPERTS = 8
EPS = 1e-6
NEG_BIG = -1e30
LOG2E = math.log2(math.e)
MLA_Q_SCALE = MLA_SCALE * LOG2E
SB_Q_SCALE = SB_SCALE * LOG2E

LANES = 128
SUBLANES = 8
VMEM_LIMIT_BYTES = 56 * 1024 * 1024

ROPE_PAD = LANES - ROPE_DIM
SEG_CQ = (0, 256)
SEG_CKV = (256, 384)
SEG_KROPE = (384, 512)
SEG_U = (512, 1024)
SEG_QSB = (1024, 1536)
SEG_KSB = (1536, 1664)
SEG_VSB = (1664, 1792)
SEG_GATES = (1792, 4864)
IN_PADDED = 4864

PAGE_KEYS = 128
SAMPLE_PAGES_PER_STEP = 32
SB_KEY_BLOCK = 256
MLA_KEY_BLOCK = 256
MLA_Q_POS_BLOCK = 256
MOE_ROW_TILE = 1024
ROW_COPY_UNROLL = 8
MLA_ROW_SUB = 128
MLA_SCORE_LOOKAHEAD = 4
Q_POS_BLOCK = 128


def _params(*sem):
    return pltpu.CompilerParams(dimension_semantics=sem, vmem_limit_bytes=VMEM_LIMIT_BYTES)


def _pick_tile(n, pref):
    t = min(n, pref)
    while n % t:
        t -= SUBLANES
    return t


def _rms(x, g):
    return x * lax.rsqrt(jnp.mean(x * x, axis=-1, keepdims=True) + EPS) * g


def _dot(a, b):
    return jnp.dot(a, b, preferred_element_type=F32)


def _dot_nt(a, b):
    return lax.dot_general(a, b, (((1,), (1,)), ((), ())), preferred_element_type=F32)


def _const_spec(shape):
    nd = len(shape)
    return pl.BlockSpec(shape, lambda *_: (0,) * nd)


def _in_proj_kernel(x_ref, gmix_ref, win_ref, gq_ref, gkv_ref, wuq_ref, wukbd_ref, perm_ref,
                    cos8_ref, sin8_ref, ka_ref, kb_ref, kc_ref,
                    qcat_ref, kcat_ref, lat_ref, kr_ref, u_ref, qsb0_ref, qsb1_ref,
                    ksb_ref, vsb_ref, gates_ref):
    h = _rms(x_ref[...], gmix_ref[...]).astype(BF)

    def seg(bounds):
        return _dot(h, win_ref[:, bounds[0]:bounds[1]])

    qn = _rms(seg(SEG_CQ), gq_ref[...]).astype(BF)
    q = _dot(qn, wuq_ref[...])
    n_nope = MLA_HEADS * QK_NOPE
    x1 = q[:, n_nope:n_nope + LANES]
    x2 = q[:, n_nope + LANES:n_nope + 2 * LANES]
    c8 = cos8_ref[...]
    s8 = sin8_ref[...]
    rot = (jnp.concatenate([x1 * c8 - x2 * s8, x1 * s8 + x2 * c8], axis=1) * MLA_Q_SCALE).astype(BF)
    qr = _dot(rot, perm_ref[...])
    for pair in range(MLA_HEADS // 2):
        qn_pair = q[:, LANES * pair:LANES * (pair + 1)].astype(BF)
        ql = _dot(qn_pair, wukbd_ref[pair]) * MLA_Q_SCALE
        for j in range(2):
            hd = 2 * pair + j
            qcat_ref[:, 256 * hd:256 * hd + LANES] = ql[:, LANES * j:LANES * (j + 1)].astype(BF)
            qcat_ref[:, 256 * hd + LANES:256 * (hd + 1)] = qr[:, LANES * hd:LANES * (hd + 1)].astype(BF)

    lat = _rms(seg(SEG_CKV), gkv_ref[...])
    lat_ref[...] = lat
    zk = seg(SEG_KROPE)
    half = ROPE_DIM // 2
    kr = (zk * ka_ref[...] + pltpu.roll(zk, LANES - half, 1) * kb_ref[...]
          + pltpu.roll(zk, half, 1) * kc_ref[...])
    kr_ref[...] = kr[:, :ROPE_DIM]
    kcat_ref[:, :LANES] = lat.astype(BF)
    kcat_ref[:, LANES:] = kr.astype(BF)

    u_ref[...] = seg(SEG_U)
    qs = seg(SEG_QSB) * SB_Q_SCALE
    hw = SB_GROUP * SB_HEAD_DIM
    qsb0_ref[...] = qs[:, :hw].astype(BF)
    qsb1_ref[...] = qs[:, hw:].astype(BF)
    ksb_ref[...] = seg(SEG_KSB)
    vsb_ref[...] = seg(SEG_VSB)
    g0 = SEG_GATES[0]
    d = (SEG_GATES[1] - g0) // 3
    for j in range(3):
        gates_ref[:, d * j:d * (j + 1)] = jax.nn.sigmoid(seg((g0 + d * j, g0 + d * (j + 1)))).astype(BF)


def _in_proj(x, lw, tabs, tm):
    n, d = x.shape
    nblk = tabs[0].shape[0] // tm
    row = lambda w: pl.BlockSpec((tm, w), lambda i: (i, 0))
    tab = pl.BlockSpec((tm, LANES), lambda i: (i % nblk, 0))
    hw = SB_GROUP * SB_HEAD_DIM
    kvw = SB_KV_HEADS * SB_HEAD_DIM
    out_shape = (
        jax.ShapeDtypeStruct((n, MLA_HEADS * 256), BF),
        jax.ShapeDtypeStruct((n, 256), BF),
        jax.ShapeDtypeStruct((n, KV_RANK), F32),
        jax.ShapeDtypeStruct((n, ROPE_DIM), F32),
        jax.ShapeDtypeStruct((n, SSM_WIDTH), F32),
        jax.ShapeDtypeStruct((n, hw), BF),
        jax.ShapeDtypeStruct((n, hw), BF),
        jax.ShapeDtypeStruct((n, kvw), F32),
        jax.ShapeDtypeStruct((n, kvw), F32),
        jax.ShapeDtypeStruct((n, 3 * d), BF),
    )
    out_specs = tuple(row(s.shape[1]) for s in out_shape)
    consts = [lw['g_mix'], lw['w_in_p'], lw['g_q'], lw['g_kv'], lw['w_uq_p'], lw['w_uk_bd'], lw['perm']]
    return pl.pallas_call(
        _in_proj_kernel,
        grid=(n // tm,),
        in_specs=[row(d)] + [_const_spec(c.shape) for c in consts] + [tab] * 5,
        out_specs=out_specs,
        out_shape=out_shape,
        compiler_params=_params("parallel"),
        name="in_proj",
    )(x, *consts, *tabs)


def _mla_prompt_kernel(qi_ref, kj_ref, last_ref, q_ref, k_ref, o_ref, m_sc, acc_sc, *, tq, kb, rs):
    p = pl.program_id(1)
    qi = qi_ref[p]
    kj = kj_ref[p]
    rows = tq * MLA_HEADS

    @pl.when(kj == 0)
    def _():
        m_sc[...] = jnp.full(m_sc.shape, NEG_BIG, F32)
        acc_sc[...] = jnp.zeros(acc_sc.shape, F32)

    k = k_ref[...]
    v_ext = jnp.concatenate([k[:, :KV_RANK], jnp.ones((kb, LANES), BF)], axis=1)

    def process(masked):
        n_sub = rows // rs
        pending = []
        for step in range(n_sub + MLA_SCORE_LOOKAHEAD):
            if step < n_sub:
                pending.append(_dot_nt(q_ref[step * rs:(step + 1) * rs, :], k))
            r = step - MLA_SCORE_LOOKAHEAD
            if r < 0:
                continue
            sl = slice(r * rs, (r + 1) * rs)
            s = pending.pop(0)
            if masked:
                row = lax.broadcasted_iota(jnp.int32, s.shape, 0)
                col = lax.broadcasted_iota(jnp.int32, s.shape, 1)
                q_pos = qi * tq + (r * rs + row) // MLA_HEADS
                s = jnp.where(kj * kb + col <= q_pos, s, NEG_BIG)
            m_prev = m_sc[sl, :]
            m_new = jnp.maximum(m_prev, jnp.max(s, axis=1, keepdims=True))
            alpha = jnp.exp2(m_prev - m_new)
            pr = jnp.exp2(s - m_new).astype(BF)
            acc_sc[sl, :] = alpha * acc_sc[sl, :] + _dot(pr, v_ext)
            m_sc[sl, :] = m_new

    needs_mask = kj * kb + kb - 1 > qi * tq

    @pl.when(needs_mask)
    def _():
        process(True)

    @pl.when(jnp.logical_not(needs_mask))
    def _():
        process(False)

    @pl.when(last_ref[p] == 1)
    def _():
        acc = acc_sc[...]
        o_ref[...] = (acc[:, :KV_RANK] / acc[:, KV_RANK:KV_RANK + 1]).astype(BF)


def _causal_pairs(t, tq, kb, strict, descending):
    qi_l, kj_l, first_l, last_l = [], [], [], []
    for qi in range(t // tq):
        last_key = qi * tq + tq - 1 - (1 if strict else 0)
        ks = list(range(last_key // kb + 1))
        if descending:
            ks = ks[::-1]
        for n, kj in enumerate(ks):
            qi_l.append(qi)
            kj_l.append(kj)
            first_l.append(1 if n == 0 else 0)
            last_l.append(1 if n == len(ks) - 1 else 0)
    as_arr = lambda v: jnp.asarray(np.asarray(v, np.int32))
    return as_arr(qi_l), as_arr(kj_l), as_arr(first_l), as_arr(last_l)


def _mla_prompt(qcat, kcat, bsz, t):
    tq = min(MLA_Q_POS_BLOCK, t)
    kb = min(MLA_KEY_BLOCK, t)
    qi_t, kj_t, _, last_t = _causal_pairs(t, tq, kb, strict=False, descending=False)
    rows = tq * MLA_HEADS
    rs = min(MLA_ROW_SUB, rows)
    q2 = qcat.reshape(bsz * t * MLA_HEADS, 256)
    nq, nk = t // tq, t // kb
    grid_spec = pltpu.PrefetchScalarGridSpec(
        num_scalar_prefetch=3,
        grid=(bsz, int(qi_t.shape[0])),
        in_specs=[
            pl.BlockSpec((rows, 256), lambda b, p, qi, kj, la: (b * nq + qi[p], 0)),
            pl.BlockSpec((kb, 256), lambda b, p, qi, kj, la: (b * nk + kj[p], 0)),
        ],
        out_specs=pl.BlockSpec((rows, KV_RANK), lambda b, p, qi, kj, la: (b * nq + qi[p], 0)),
        scratch_shapes=[pltpu.VMEM((rows, 1), F32), pltpu.VMEM((rows, 2 * LANES), F32)],
    )
    o = pl.pallas_call(
        functools.partial(_mla_prompt_kernel, tq=tq, kb=kb, rs=rs),
        grid_spec=grid_spec,
        out_shape=jax.ShapeDtypeStruct((bsz * t * MLA_HEADS, KV_RANK), BF),
        compiler_params=_params("parallel", "arbitrary"),
        name="mla_prompt",
    )(qi_t, kj_t, last_t, q2, kcat)
    return o.reshape(bsz * t, MLA_HEADS * KV_RANK)


def _stream_page_chunks(pt_ref, pools, bufs, sems, chunk_of_step, layer, pb, page_axis, compute):
    b = pl.program_id(0)
    j = pl.program_id(1)
    nb = pl.num_programs(0)
    nch = pl.num_programs(1) - 1

    def copies(bb, jj, slot):
        chunk = chunk_of_step(jj)
        out = []
        for i in range(pb):
            page = pt_ref[bb, chunk * pb + i]
            for a, (pool, buf) in enumerate(zip(pools, bufs)):
                extent = pool.shape[2 + page_axis[a]]
                window = pl.ds(i * extent, extent)
                dst = buf.at[slot, window, :] if page_axis[a] == 0 else buf.at[slot, :, window]
                out.append(pltpu.make_async_copy(pool.at[layer, page], dst, sems.at[a, slot]))
        return out

    @pl.when((j == 0) & (b == 0))
    def _():
        for cp in copies(0, 1, 0):
            cp.start()

    @pl.when(j > 0)
    def _():
        slot = (b * nch + j - 1) % 2
        for cp in copies(b, j, slot):
            cp.wait()

        @pl.when(j < nch)
        def _():
            for cp in copies(b, j + 1, 1 - slot):
                cp.start()

        @pl.when((j == nch) & (b + 1 < nb))
        def _():
            for cp in copies(b + 1, 1, 1 - slot):
                cp.start()

        compute(slot)


def _mla_sample_kernel(pt_ref, q_ref, newk_ref, lat_hbm, kr_hbm, o_ref, lat_buf, kr_buf, sems,
                       m_sc, l_sc, acc_sc, *, pb, layer):
    j = pl.program_id(1)
    q = q_ref[...]

    def update(s, vals):
        m_prev = m_sc[...]
        m_new = jnp.maximum(m_prev, jnp.max(s, axis=1, keepdims=True))
        alpha = jnp.exp2(m_prev - m_new)
        pr = jnp.exp2(s - m_new)
        l_sc[...] = alpha * l_sc[...] + jnp.sum(pr, axis=1, keepdims=True)
        acc_sc[...] = alpha * acc_sc[...] + _dot(pr.astype(BF), vals)
        m_sc[...] = m_new

    @pl.when(j == 0)
    def _():
        m_sc[...] = jnp.full(m_sc.shape, NEG_BIG, F32)
        l_sc[...] = jnp.zeros(l_sc.shape, F32)
        acc_sc[...] = jnp.zeros(acc_sc.shape, F32)
        nk = newk_ref[...]
        s = _dot_nt(q, nk)
        row = lax.broadcasted_iota(jnp.int32, s.shape, 0)
        col = lax.broadcasted_iota(jnp.int32, s.shape, 1)
        s = jnp.where(col <= row // MLA_HEADS, s, NEG_BIG)
        update(s, nk[:, :KV_RANK])

    def paged(slot):
        q_lat = q[:, :KV_RANK]
        q_rope = q[:, KV_RANK:KV_RANK + ROPE_DIM]
        lat = lat_buf[slot].astype(BF)
        kr_t = kr_buf[slot].astype(BF)
        update(_dot_nt(q_lat, lat) + _dot(q_rope, kr_t), lat)

    _stream_page_chunks(pt_ref, (lat_hbm, kr_hbm), (lat_buf, kr_buf), sems, lambda jj: jj - 1,
                        layer, pb, (0, 1), paged)

    @pl.when(j == pl.num_programs(1) - 1)
    def _():
        o_ref[...] = (acc_sc[...] / l_sc[...]).astype(BF)


def _mla_sample(qcat, kcat, lat_pool, kr_pool, page_table, layer, bsz, t):
    n_pages = page_table.shape[1]
    pb = min(SAMPLE_PAGES_PER_STEP, n_pages)
    nch = n_pages // pb
    rows = t * MLA_HEADS
    q3 = qcat.reshape(bsz, rows, 256)
    newk = jnp.pad(kcat.reshape(bsz, t, 256), ((0, 0), (0, PAGE_KEYS - t), (0, 0)))
    kr_pool = jnp.swapaxes(kr_pool, 2, 3)
    hbm = pl.BlockSpec(memory_space=pl.ANY)
    grid_spec = pltpu.PrefetchScalarGridSpec(
        num_scalar_prefetch=1,
        grid=(bsz, nch + 1),
        in_specs=[pl.BlockSpec((None, rows, 256), lambda b, j, pt: (b, 0, 0)),
                  pl.BlockSpec((None, PAGE_KEYS, 256), lambda b, j, pt: (b, 0, 0)),
                  hbm, hbm],
        out_specs=pl.BlockSpec((None, rows, KV_RANK), lambda b, j, pt: (b, 0, 0)),
        scratch_shapes=[pltpu.VMEM((2, pb * PAGE_KEYS, KV_RANK), F32),
                        pltpu.VMEM((2, ROPE_DIM, pb * PAGE_KEYS), F32),
                        pltpu.SemaphoreType.DMA((2, 2)),
                        pltpu.VMEM((rows, 1), F32), pltpu.VMEM((rows, 1), F32),
                        pltpu.VMEM((rows, KV_RANK), F32)],
    )
    o = pl.pallas_call(
        functools.partial(_mla_sample_kernel, pb=pb, layer=layer),
        grid_spec=grid_spec,
        out_shape=jax.ShapeDtypeStruct((bsz, rows, KV_RANK), BF),
        compiler_params=_params("arbitrary", "arbitrary"),
        name="mla_sample",
    )(page_table, q3, newk, lat_pool, kr_pool)
    return o.reshape(bsz * t, MLA_HEADS * KV_RANK)


def _sb_log2_1m(z, mask):
    log_1m = -(jnp.maximum(z, 0.0) + jnp.log2(1.0 + jnp.exp2(-jnp.abs(z))))
    if mask is not None:
        log_1m = jnp.where(mask, log_1m, 0.0)
    return log_1m


def _suffix_sums(log_1m, tri, two_terms=True):
    hi = log_1m.astype(BF)
    if not two_terms:
        return _dot(hi, tri)
    lo = (log_1m - hi.astype(F32)).astype(BF)
    return _dot(hi, tri) + _dot(lo, tri)


def _sb_weights(z, log_1m, between, mask):
    a = jnp.exp2(z + log_1m + between)
    if mask is not None:
        a = jnp.where(mask, a, 0.0)
    return a.astype(BF)


def _sb_prompt_kernel(qi_ref, kj_ref, first_ref, last_ref, q0_ref, q1_ref, k_ref, v_ref, tri_ref,
                      o0_ref, o1_ref, carry_sc, acc_sc, *, tq, kb):
    p = pl.program_id(1)
    qi = qi_ref[p]
    kj = kj_ref[p]

    @pl.when(first_ref[p] == 1)
    def _():
        carry_sc[...] = jnp.zeros(carry_sc.shape, F32)
        acc_sc[...] = jnp.zeros(acc_sc.shape, F32)

    rows = tq * SB_GROUP

    def process(masked):
        mask = None
        if masked:
            row = lax.broadcasted_iota(jnp.int32, (rows, kb), 0)
            col = lax.broadcasted_iota(jnp.int32, (rows, kb), 1)
            mask = (kj * kb + col) < (qi * tq + row // SB_GROUP)
        tri = tri_ref[...]
        kf = k_ref[...]
        vf = v_ref[...]
        for kv, q_ref in enumerate((q0_ref, q1_ref)):
            k = kf[:, SB_HEAD_DIM * kv:SB_HEAD_DIM * (kv + 1)].astype(BF)
            v = vf[:, SB_HEAD_DIM * kv:SB_HEAD_DIM * (kv + 1)].astype(BF)
            z = _dot_nt(q_ref[...], k)
            log_1m = _sb_log2_1m(z, mask)
            w = _suffix_sums(log_1m, tri, two_terms=False)
            carry = carry_sc[kv]
            a = _sb_weights(z, log_1m, carry + w, mask)
            acc_sc[kv] = acc_sc[kv] + _dot(a, v)
            carry_sc[kv] = carry + (w + log_1m)[:, 0:1]

    needs_mask = kj * kb + kb - 1 >= qi * tq

    @pl.when(needs_mask)
    def _():
        process(True)

    @pl.when(jnp.logical_not(needs_mask))
    def _():
        process(False)

    @pl.when(last_ref[p] == 1)
    def _():
        o0_ref[...] = acc_sc[0].astype(BF)
        o1_ref[...] = acc_sc[1].astype(BF)


def _strict_lower_ones(n):
    j = np.arange(n)[:, None]
    s = np.arange(n)[None, :]
    return jnp.asarray((j > s).astype(np.float32), BF)


def _sb_prompt(qsb0, qsb1, ksb, vsb, bsz, t):
    tq = min(Q_POS_BLOCK, t)
    kb = min(SB_KEY_BLOCK, t)
    qi_t, kj_t, first_t, last_t = _causal_pairs(t, tq, kb, strict=True, descending=True)
    rows = tq * SB_GROUP
    nq, nk = t // tq, t // kb
    n = bsz * t
    q0 = qsb0.reshape(n * SB_GROUP, SB_HEAD_DIM)
    q1 = qsb1.reshape(n * SB_GROUP, SB_HEAD_DIM)
    kvw = SB_KV_HEADS * SB_HEAD_DIM
    qspec = pl.BlockSpec((rows, SB_HEAD_DIM), lambda b, p, qi, kj, fi, la: (b * nq + qi[p], 0))
    kspec = pl.BlockSpec((kb, kvw), lambda b, p, qi, kj, fi, la: (b * nk + kj[p], 0))
    grid_spec = pltpu.PrefetchScalarGridSpec(
        num_scalar_prefetch=4,
        grid=(bsz, int(qi_t.shape[0])),
        in_specs=[qspec, qspec, kspec, kspec,
                  pl.BlockSpec((kb, kb), lambda b, p, qi, kj, fi, la: (0, 0))],
        out_specs=(qspec, qspec),
        scratch_shapes=[pltpu.VMEM((SB_KV_HEADS, rows, 1), F32),
                        pltpu.VMEM((SB_KV_HEADS, rows, SB_HEAD_DIM), F32)],
    )
    o_sds = jax.ShapeDtypeStruct((n * SB_GROUP, SB_HEAD_DIM), BF)
    o0, o1 = pl.pallas_call(
        functools.partial(_sb_prompt_kernel, tq=tq, kb=kb),
        grid_spec=grid_spec,
        out_shape=(o_sds, o_sds),
        compiler_params=_params("parallel", "arbitrary"),
        name="sb_prompt",
    )(qi_t, kj_t, first_t, last_t, q0, q1, ksb, vsb, _strict_lower_ones(kb))
    hw = SB_GROUP * SB_HEAD_DIM
    return o0.reshape(n, hw), o1.reshape(n, hw)


def _sb_sample_kernel(pt_ref, q_ref, newk_ref, newv_ref, tri_new_ref, tri_ref, k_hbm, v_hbm,
                      o_ref, k_buf, v_buf, sems, carry_sc, acc_sc, *, pb, t_new, layer):
    j = pl.program_id(1)
    q = q_ref[...]
    rows = q.shape[0]
    rows_per_kv = t_new * SB_GROUP

    @pl.when(j == 0)
    def _():
        nk_t = newk_ref[...].astype(BF)
        nv_t = newv_ref[...].astype(BF)
        row = lax.broadcasted_iota(jnp.int32, (rows, PAGE_KEYS), 0)
        col = lax.broadcasted_iota(jnp.int32, (rows, PAGE_KEYS), 1)
        mask = col < (row % rows_per_kv) // SB_GROUP
        z = _dot(q, nk_t)
        log_1m = _sb_log2_1m(z, mask)
        w = _suffix_sums(log_1m, tri_new_ref[...])
        a = _sb_weights(z, log_1m, w, mask)
        acc_sc[...] = _dot_nt(a, nv_t)
        carry_sc[...] = (w + log_1m)[:, 0:1]

    def paged(slot):
        k_t = k_buf[slot].astype(BF)
        v_t = v_buf[slot].astype(BF)
        z = _dot(q, k_t)
        log_1m = _sb_log2_1m(z, None)
        sub = tri_ref.shape[0]
        nsub = z.shape[1] // sub
        stacked = jnp.concatenate([log_1m[:, sub * s:sub * (s + 1)] for s in range(nsub)], axis=0)
        w = _suffix_sums(stacked, tri_ref[...])
        totals = (w + stacked)[:, 0:1]
        carry = carry_sc[...]
        between = [None] * nsub
        for s in reversed(range(nsub)):
            between[s] = carry + w[rows * s:rows * (s + 1), :]
            carry = carry + totals[rows * s:rows * (s + 1), :]
        a = _sb_weights(z, log_1m, jnp.concatenate(between, axis=1), None)
        acc_sc[...] = acc_sc[...] + _dot_nt(a, v_t)
        carry_sc[...] = carry

    _stream_page_chunks(pt_ref, (k_hbm, v_hbm), (k_buf, v_buf), sems,
                        lambda jj: pl.num_programs(1) - 1 - jj, layer, pb, (1, 1), paged)

    @pl.when(j == pl.num_programs(1) - 1)
    def _():
        o_ref[...] = acc_sc[...]


def _sb_sample(qsb0, qsb1, ksb, vsb, k_pool, v_pool, page_table, layer, bsz, t):
    n_pages = page_table.shape[1]
    pb = min(SAMPLE_PAGES_PER_STEP, n_pages)
    nch = n_pages // pb
    kvw = SB_KV_HEADS * SB_HEAD_DIM
    rows = SB_KV_HEADS * t * SB_GROUP
    sub = min(SB_KEY_BLOCK, pb * PAGE_KEYS)
    q0 = qsb0.reshape(bsz, t * SB_GROUP, SB_HEAD_DIM)
    q1 = qsb1.reshape(bsz, t * SB_GROUP, SB_HEAD_DIM)
    zq = jnp.zeros_like(q0)
    q = jnp.concatenate([jnp.concatenate([q0, zq], axis=2), jnp.concatenate([zq, q1], axis=2)], axis=1)
    pad = ((0, 0), (0, 0), (0, PAGE_KEYS - t))
    newk = jnp.pad(jnp.swapaxes(ksb.reshape(bsz, t, kvw), 1, 2), pad)
    newv = jnp.pad(jnp.swapaxes(vsb.reshape(bsz, t, kvw), 1, 2), pad)
    depth, n_phys = k_pool.shape[0], k_pool.shape[1]
    kp = jnp.transpose(k_pool, (0, 1, 3, 4, 2)).reshape(depth, n_phys, kvw, PAGE_KEYS)
    vp = jnp.transpose(v_pool, (0, 1, 3, 4, 2)).reshape(depth, n_phys, kvw, PAGE_KEYS)

    per_b = lambda r, w: pl.BlockSpec((None, r, w), lambda b, j, pt: (b, 0, 0))
    const2 = lambda r: pl.BlockSpec((r, r), lambda b, j, pt: (0, 0))
    hbm = pl.BlockSpec(memory_space=pl.ANY)
    grid_spec = pltpu.PrefetchScalarGridSpec(
        num_scalar_prefetch=1,
        grid=(bsz, nch + 1),
        in_specs=[per_b(rows, kvw), per_b(kvw, PAGE_KEYS), per_b(kvw, PAGE_KEYS),
                  const2(PAGE_KEYS), const2(sub), hbm, hbm],
        out_specs=per_b(rows, kvw),
        scratch_shapes=[pltpu.VMEM((2, kvw, pb * PAGE_KEYS), F32),
                        pltpu.VMEM((2, kvw, pb * PAGE_KEYS), F32),
                        pltpu.SemaphoreType.DMA((2, 2)),
                        pltpu.VMEM((rows, 1), F32), pltpu.VMEM((rows, kvw), F32)],
    )
    o = pl.pallas_call(
        functools.partial(_sb_sample_kernel, pb=pb, t_new=t, layer=layer),
        grid_spec=grid_spec,
        out_shape=jax.ShapeDtypeStruct((bsz, rows, kvw), F32),
        compiler_params=_params("arbitrary", "arbitrary"),
        name="sb_sample",
    )(page_table, q, newk, newv, _strict_lower_ones(PAGE_KEYS), _strict_lower_ones(sub), kp, vp)
    o = o.reshape(bsz, SB_KV_HEADS, t * SB_GROUP, SB_KV_HEADS, SB_HEAD_DIM)
    hw = SB_GROUP * SB_HEAD_DIM
    o0 = o[:, 0, :, 0, :].reshape(bsz * t, hw).astype(BF)
    o1 = o[:, 1, :, 1, :].reshape(bsz * t, hw).astype(BF)
    return o0, o1


def _s5_kernel(u_ref, h0re_ref, h0im_ref, are_ref, aim_ref, bbd_ref, cbd_ref, dskip_ref, wglu_ref,
               bglu_ref, ob_ref, hre_ref, him_ref, s_sc, st_sc, *, bn, tc, lc):
    c = pl.program_id(0)

    @pl.when(c == 0)
    def _():
        st_sc[0] = h0re_ref[...]
        st_sc[1] = h0im_ref[...]

    u = u_ref[...]
    ub = u.astype(BF)
    nflat = SSM_FLAT
    for jb in range(2 * nflat // 512):
        s_sc[:, 512 * jb:512 * (jb + 1)] = _dot(ub, bbd_ref[:, 512 * jb:512 * (jb + 1)])

    for ci in range(nflat // lc):
        lo = ci * lc
        ar = jnp.broadcast_to(are_ref[:, lo:lo + lc], (bn, lc))
        ai = jnp.broadcast_to(aim_ref[:, lo:lo + lc], (bn, lc))

        def body(t, carry, lo=lo, ar=ar, ai=ai):
            hr, hi = carry
            r0 = pl.multiple_of(t * bn, bn)
            br = s_sc[pl.ds(r0, bn), lo:lo + lc]
            bi = s_sc[pl.ds(r0, bn), nflat + lo:nflat + lo + lc]
            nr = ar * hr - ai * hi + br
            ni = ar * hi + ai * hr + bi
            s_sc[pl.ds(r0, bn), lo:lo + lc] = nr
            s_sc[pl.ds(r0, bn), nflat + lo:nflat + lo + lc] = ni
            return nr, ni

        hr, hi = lax.fori_loop(0, tc, body, (st_sc[0, :, lo:lo + lc], st_sc[1, :, lo:lo + lc]))
        st_sc[0, :, lo:lo + lc] = hr
        st_sc[1, :, lo:lo + lc] = hi

    y = dskip_ref[...] * u
    for jb in range(2 * nflat // 1024):
        y = y + _dot(s_sc[:, 1024 * jb:1024 * (jb + 1)].astype(BF), cbd_ref[1024 * jb:1024 * (jb + 1), :])
    g = jax.nn.gelu(y)
    gate = jax.nn.sigmoid(_dot(g.astype(BF), wglu_ref[...]) + bglu_ref[...])
    ob_ref[...] = (g * gate).astype(BF)

    @pl.when(c == pl.num_programs(0) - 1)
    def _():
        hre_ref[...] = st_sc[0]
        him_ref[...] = st_sc[1]


def _s5(u_tm, h0_re, h0_im, lw, bn, t):
    tc = t
    while tc * bn > 512 and tc % 2 == 0:
        tc //= 2
    rows = tc * bn
    lc = 256 if bn <= 16 else LANES
    nflat = SSM_FLAT
    consts = [lw['ssm_a_re'], lw['ssm_a_im'], lw['ssm_bbd'], lw['ssm_cbd'], lw['ssm_d'],
              lw['ssm_w_glu'], lw['ssm_b_glu']]
    st_spec = pl.BlockSpec((bn, nflat), lambda c: (0, 0))
    st_sds = jax.ShapeDtypeStruct((bn, nflat), F32)
    return pl.pallas_call(
        functools.partial(_s5_kernel, bn=bn, tc=tc, lc=lc),
        grid=(t // tc,),
        in_specs=[pl.BlockSpec((rows, SSM_WIDTH), lambda c: (c, 0)), st_spec, st_spec]
                 + [_const_spec(a.shape) for a in consts],
        out_specs=(pl.BlockSpec((rows, SSM_WIDTH), lambda c: (c, 0)), st_spec, st_spec),
        out_shape=(jax.ShapeDtypeStruct((t * bn, SSM_WIDTH), BF), st_sds, st_sds),
        scratch_shapes=[pltpu.VMEM((rows, 2 * nflat), F32), pltpu.VMEM((2, bn, nflat), F32)],
        compiler_params=_params("arbitrary"),
        name="s5",
    )(u_tm, h0_re, h0_im, *consts)


def _merge_kernel(x_ref, olat_ref, ob_ref, oc0_ref, oc1_ref, gates_ref, wuv_ref, wa_ref, wb_ref,
                  wc_ref, wout_ref, gffn_ref, *rest, routed):
    if routed:
        router_ref, x1_ref, hf_ref, route_ref = rest
    else:
        x1_ref, hf_ref = rest
    d = x_ref.shape[1]
    hw = SB_GROUP * SB_HEAD_DIM
    o_a = _dot(olat_ref[...], wuv_ref[...]).astype(BF)
    br_a = _dot(o_a, wa_ref[...])
    br_b = _dot(ob_ref[...], wb_ref[...])
    br_c = _dot(oc0_ref[...], wc_ref[:hw, :]) + _dot(oc1_ref[...], wc_ref[hw:, :])
    merged = (gates_ref[:, :d].astype(F32) * br_a + gates_ref[:, d:2 * d].astype(F32) * br_b
              + gates_ref[:, 2 * d:].astype(F32) * br_c)
    x1 = x_ref[...] + _dot(merged.astype(BF), wout_ref[...])
    x1_ref[...] = x1
    hf = _rms(x1, gffn_ref[...])
    hf_ref[...] = hf.astype(hf_ref.dtype)
    if routed:
        router = router_ref[...]
        r_hi = router.astype(BF)
        r_lo = (router - r_hi.astype(F32)).astype(BF)
        h_hi = hf.astype(BF)
        h_lo = (hf - h_hi.astype(F32)).astype(BF)
        logits = _dot(h_hi, r_hi) + (_dot(h_hi, r_lo) + _dot(h_lo, r_hi))
        lane = lax.broadcasted_iota(jnp.int32, logits.shape, 1)
        lg = jnp.where(lane < N_EXPERTS, logits, NEG_BIG)
        v1 = jnp.max(lg, axis=1, keepdims=True)
        i1 = jnp.min(jnp.where(lg == v1, lane, LANES), axis=1, keepdims=True)
        lg2 = jnp.where(lane == i1, NEG_BIG, lg)
        v2 = jnp.max(lg2, axis=1, keepdims=True)
        i2 = jnp.min(jnp.where(lg2 == v2, lane, LANES), axis=1, keepdims=True)
        e2 = jnp.exp(v2 - v1)
        p1 = 1.0 / (1.0 + e2)
        p2 = e2 / (1.0 + e2)
        route_ref[...] = (jnp.where(lane == 0, i1.astype(F32), 0.0) + jnp.where(lane == 1, i2.astype(F32), 0.0)
                          + jnp.where(lane == 2, p1, 0.0) + jnp.where(lane == 3, p2, 0.0))


def _merge(x, olat, ob, oc0, oc1, gates, lw, router, tm):
    n, d = x.shape
    routed = router is not None
    row = lambda w: pl.BlockSpec((tm, w), lambda i: (i, 0))
    consts = [lw['w_uv_bd'], lw['w_br_a'], lw['w_br_b'], lw['w_br_c'], lw['w_out'], lw['g_ffn']]
    if routed:
        consts.append(router)
    acts = [x, olat, ob, oc0, oc1, gates]
    out_shape = [jax.ShapeDtypeStruct((n, d), F32), jax.ShapeDtypeStruct((n, d), F32 if routed else BF)]
    if routed:
        out_shape.append(jax.ShapeDtypeStruct((n, LANES), F32))
    return pl.pallas_call(
        functools.partial(_merge_kernel, routed=routed),
        grid=(n // tm,),
        in_specs=[row(a.shape[1]) for a in acts] + [_const_spec(c.shape) for c in consts],
        out_specs=tuple(row(s.shape[1]) for s in out_shape),
        out_shape=tuple(out_shape),
        compiler_params=_params("parallel"),
        name="merge",
    )(*acts, *consts)


def _ffn_kernel(h_ref, x_ref, wg_ref, wu_ref, wd_ref, o_ref, acc_sc):
    f = pl.program_id(1)

    @pl.when(f == 0)
    def _():
        acc_sc[...] = jnp.zeros(acc_sc.shape, F32)

    h = h_ref[...]
    act = (jax.nn.silu(_dot(h, wg_ref[...])) * _dot(h, wu_ref[...])).astype(BF)
    acc_sc[...] += _dot(act, wd_ref[...])

    @pl.when(f == pl.num_programs(1) - 1)
    def _():
        o_ref[...] = x_ref[...] + acc_sc[...]


def _ffn(hf, x, wg, wu, wd, tm, tf):
    n, d = x.shape
    dff = wg.shape[1]
    return pl.pallas_call(
        _ffn_kernel,
        grid=(n // tm, dff // tf),
        in_specs=[pl.BlockSpec((tm, d), lambda i, f: (i, 0)),
                  pl.BlockSpec((tm, d), lambda i, f: (i, 0)),
                  pl.BlockSpec((d, tf), lambda i, f: (0, f)),
                  pl.BlockSpec((d, tf), lambda i, f: (0, f)),
                  pl.BlockSpec((tf, d), lambda i, f: (f, 0))],
        out_specs=pl.BlockSpec((tm, d), lambda i, f: (i, 0)),
        out_shape=jax.ShapeDtypeStruct((n, d), F32),
        scratch_shapes=[pltpu.VMEM((tm, d), F32)],
        compiler_params=_params("parallel", "arbitrary"),
        name="ffn_dense",
    )(hf, x, wg, wu, wd)


def _route_plan(route, tr):
    n = route.shape[0]
    experts = route[:, :2].astype(jnp.int32)
    hot = (experts[:, :, None] == jnp.arange(N_EXPERTS, dtype=jnp.int32)[None, None, :]).astype(jnp.int32).sum(1)
    padded = ((hot.sum(0) + tr - 1) // tr) * tr
    ends = jnp.cumsum(padded)
    dest = (ends - padded)[None, :] + jnp.cumsum(hot, axis=0) - hot
    pos = jnp.take_along_axis(dest, experts, axis=1).astype(jnp.int32)
    n_tiles = (2 * n) // tr + N_EXPERTS
    starts = jnp.arange(n_tiles, dtype=jnp.int32) * tr
    tile_expert = jnp.minimum(jnp.searchsorted(ends, starts, side='right'), N_EXPERTS - 1).astype(jnp.int32)
    n_used = (ends[-1] // tr).astype(jnp.int32).reshape(1)
    return pos[:, 0], pos[:, 1], tile_expert, n_used, n_tiles


def _row_copy_loop(n_rows, make_copies, wait):
    def body(t, carry):
        for cp in make_copies(t):
            if wait:
                cp.wait()
            else:
                cp.start()
        return carry

    lax.fori_loop(0, n_rows, body, 0, unroll=ROW_COPY_UNROLL)


def _dispatch_kernel(pos1_ref, pos2_ref, h_ref, xs_init_ref, xs_ref, sem, *, tm):
    del xs_init_ref
    base = pl.program_id(0) * tm

    def copies(t):
        src = h_ref.at[pl.ds(t, 1), :]
        return [pltpu.make_async_copy(src, xs_ref.at[pl.ds(pos_ref[base + t], 1), :], sem)
                for pos_ref in (pos1_ref, pos2_ref)]

    _row_copy_loop(tm, copies, wait=False)
    _row_copy_loop(tm, copies, wait=True)


def _dispatch(hf, pos1, pos2, n_rows, tm):
    n, d = hf.shape
    grid_spec = pltpu.PrefetchScalarGridSpec(
        num_scalar_prefetch=2,
        grid=(n // tm,),
        in_specs=[pl.BlockSpec((tm, d), lambda i, p1, p2: (i, 0)), pl.BlockSpec(memory_space=pl.ANY)],
        out_specs=pl.BlockSpec(memory_space=pl.ANY),
        scratch_shapes=[pltpu.SemaphoreType.DMA(())],
    )
    return pl.pallas_call(
        functools.partial(_dispatch_kernel, tm=tm),
        grid_spec=grid_spec,
        out_shape=jax.ShapeDtypeStruct((n_rows, d), F32),
        input_output_aliases={3: 0},
        compiler_params=_params("arbitrary"),
        name="moe_dispatch",
    )(pos1, pos2, hf, jnp.zeros((n_rows, d), F32))


def _expert_ffn_kernel(te_ref, nu_ref, xs_ref, wg_ref, wu_ref, wd_ref, ys_ref, acc_sc):
    i = pl.program_id(0)
    f = pl.program_id(1)

    @pl.when(f == 0)
    def _():
        acc_sc[...] = jnp.zeros(acc_sc.shape, F32)

    @pl.when(i < nu_ref[0])
    def _():
        h = xs_ref[...].astype(BF)
        act = (jax.nn.silu(_dot(h, wg_ref[...])) * _dot(h, wu_ref[...])).astype(BF)
        acc_sc[...] += _dot(act, wd_ref[...])

    @pl.when(f == pl.num_programs(1) - 1)
    def _():
        ys_ref[...] = acc_sc[...]


def _expert_ffn(xs, tile_expert, n_used, wg, wu, wd, tr, tf):
    n_rows, d = xs.shape
    dff = wg.shape[2]
    grid_spec = pltpu.PrefetchScalarGridSpec(
        num_scalar_prefetch=2,
        grid=(n_rows // tr, dff // tf),
        in_specs=[pl.BlockSpec((tr, d), lambda i, f, te, nu: (i, 0)),
                  pl.BlockSpec((None, d, tf), lambda i, f, te, nu: (te[i], 0, f)),
                  pl.BlockSpec((None, d, tf), lambda i, f, te, nu: (te[i], 0, f)),
                  pl.BlockSpec((None, tf, d), lambda i, f, te, nu: (te[i], f, 0))],
        out_specs=pl.BlockSpec((tr, d), lambda i, f, te, nu: (i, 0)),
        scratch_shapes=[pltpu.VMEM((tr, d), F32)],
    )
    return pl.pallas_call(
        _expert_ffn_kernel,
        grid_spec=grid_spec,
        out_shape=jax.ShapeDtypeStruct((n_rows, d), F32),
        compiler_params=_params("parallel", "arbitrary"),
        name="moe_experts",
    )(tile_expert, n_used, xs, wg, wu, wd)


def _combine_kernel(pos1_ref, pos2_ref, x_ref, route_ref, ys_ref, o_ref, buf, sems, *, tm):
    i = pl.program_id(0)
    n_steps = pl.num_programs(0)

    def copies_of(step, slot):
        def copies(t):
            return [pltpu.make_async_copy(ys_ref.at[pl.ds(pos_ref[step * tm + t], 1), :],
                                          buf.at[slot, k, pl.ds(t, 1), :], sems.at[slot])
                    for k, pos_ref in enumerate((pos1_ref, pos2_ref))]
        return copies

    @pl.when(i == 0)
    def _():
        _row_copy_loop(tm, copies_of(0, 0), wait=False)

    slot = i % 2
    _row_copy_loop(tm, copies_of(i, slot), wait=True)

    @pl.when(i + 1 < n_steps)
    def _():
        _row_copy_loop(tm, copies_of(i + 1, 1 - slot), wait=False)

    route = route_ref[...]
    o_ref[...] = x_ref[...] + route[:, 2:3] * buf[slot, 0] + route[:, 3:4] * buf[slot, 1]


def _combine(x, route, ys, pos1, pos2, tm):
    n, d = x.shape
    grid_spec = pltpu.PrefetchScalarGridSpec(
        num_scalar_prefetch=2,
        grid=(n // tm,),
        in_specs=[pl.BlockSpec((tm, d), lambda i, p1, p2: (i, 0)),
                  pl.BlockSpec((tm, LANES), lambda i, p1, p2: (i, 0)),
                  pl.BlockSpec(memory_space=pl.ANY)],
        out_specs=pl.BlockSpec((tm, d), lambda i, p1, p2: (i, 0)),
        scratch_shapes=[pltpu.VMEM((2, 2, tm, d), F32), pltpu.SemaphoreType.DMA((2,))],
    )
    return pl.pallas_call(
        functools.partial(_combine_kernel, tm=tm),
        grid_spec=grid_spec,
        out_shape=jax.ShapeDtypeStruct((n, d), F32),
        compiler_params=_params("arbitrary"),
        name="moe_combine",
    )(pos1, pos2, x, route, ys)


def _moe(hf, x, route, wg, wu, wd, tf):
    n, d = x.shape
    tr = MOE_ROW_TILE if 2 * n >= N_EXPERTS * MOE_ROW_TILE else 256
    pos1, pos2, tile_expert, n_used, n_tiles = _route_plan(route, tr)
    tm = _pick_tile(n, 256)
    xs = _dispatch(hf, pos1, pos2, n_tiles * tr, tm)
    ys = _expert_ffn(xs, tile_expert, n_used, wg, wu, wd, tr, tf)
    return _combine(x, route, ys, pos1, pos2, tm)


def _final_norm_kernel(x_ref, g_ref, o_ref):
    o_ref[...] = _rms(x_ref[...], g_ref[...])


def _final_norm(x, g, tm):
    n, d = x.shape
    return pl.pallas_call(
        _final_norm_kernel,
        grid=(n // tm,),
        in_specs=[pl.BlockSpec((tm, d), lambda i: (i, 0)), _const_spec(g.shape)],
        out_specs=pl.BlockSpec((tm, d), lambda i: (i, 0)),
        out_shape=jax.ShapeDtypeStruct((n, d), F32),
        compiler_params=_params("parallel"),
        name="final_norm",
    )(x, g)


def _prep_layer(w, l):
    d = w['w_in'].shape[1]
    lw = {}
    lw['g_mix'] = w['g_mix'][l][None, :]
    w_in = w['w_in'][l]
    split = SEG_KROPE[0] + ROPE_DIM
    lw['w_in_p'] = jnp.concatenate(
        [w_in[:, :split], jnp.zeros((d, ROPE_PAD), w_in.dtype), w_in[:, split:]], axis=1).astype(BF)
    lw['g_q'] = w['g_q'][l][None, :]
    lw['g_kv'] = w['g_kv'][l][None, :]
    half = ROPE_DIM // 2
    w3 = w['w_uq'][l].reshape(Q_RANK, MLA_HEADS, QK_NOPE + ROPE_DIM)
    lw['w_uq_p'] = jnp.concatenate(
        [w3[:, :, :QK_NOPE].reshape(Q_RANK, -1),
         w3[:, :, QK_NOPE:QK_NOPE + half].reshape(Q_RANK, -1),
         w3[:, :, QK_NOPE + half:].reshape(Q_RANK, -1)], axis=1).astype(BF)
    wk = jnp.transpose(w['w_uk'][l], (1, 2, 0))
    zk = jnp.zeros((QK_NOPE, KV_RANK), wk.dtype)
    lw['w_uk_bd'] = jnp.stack([
        jnp.concatenate([jnp.concatenate([wk[2 * p], zk], axis=1),
                         jnp.concatenate([zk, wk[2 * p + 1]], axis=1)], axis=0)
        for p in range(MLA_HEADS // 2)]).astype(BF)
    perm = np.zeros((2 * LANES, MLA_HEADS * LANES), np.float32)
    for h in range(MLA_HEADS):
        for i in range(half):
            perm[h * half + i, LANES * h + i] = 1.0
            perm[LANES + h * half + i, LANES * h + half + i] = 1.0
    lw['perm'] = jnp.asarray(perm, BF)
    eye_h = jnp.eye(MLA_HEADS, dtype=F32)
    lw['w_uv_bd'] = jnp.einsum('rhd,hg->hrgd', w['w_uv'][l], eye_h).reshape(
        MLA_HEADS * KV_RANK, MLA_HEADS * V_DIM).astype(BF)
    for name in ('w_br_a', 'w_br_b', 'w_br_c', 'w_out', 'ssm_w_glu'):
        lw[name] = w[name][l].astype(BF)
    lw['g_ffn'] = w['g_ffn'][l][None, :]

    lam_re, lam_im = w['ssm_lam_re'][l], w['ssm_lam_im'][l]
    dt = jnp.exp(w['ssm_log_dt'][l])[:, None]
    decay = jnp.exp(lam_re * dt)
    ab_re, ab_im = decay * jnp.cos(lam_im * dt), decay * jnp.sin(lam_im * dt)
    inv_den = 1.0 / (lam_re * lam_re + lam_im * lam_im)
    nr, ni = ab_re - 1.0, ab_im
    co_re = (nr * lam_re + ni * lam_im) * inv_den
    co_im = (ni * lam_re - nr * lam_im) * inv_den
    b_re, b_im = w['ssm_b_re'][l], w['ssm_b_im'][l]
    bb_re = co_re[..., None] * b_re - co_im[..., None] * b_im
    bb_im = co_re[..., None] * b_im + co_im[..., None] * b_re
    eye_g = jnp.eye(SSM_GROUPS, dtype=F32)
    bd = lambda bb: jnp.einsum('gnp,gh->gphn', bb, eye_g).reshape(SSM_WIDTH, SSM_FLAT)
    lw['ssm_bbd'] = jnp.concatenate([bd(bb_re), bd(bb_im)], axis=1).astype(BF)
    cd = lambda cc: jnp.einsum('gpn,gh->gnhp', cc, eye_g).reshape(SSM_FLAT, SSM_WIDTH)
    lw['ssm_cbd'] = jnp.concatenate([cd(w['ssm_c_re'][l]), -cd(w['ssm_c_im'][l])], axis=0).astype(BF)
    lw['ssm_a_re'] = ab_re.reshape(1, SSM_FLAT)
    lw['ssm_a_im'] = ab_im.reshape(1, SSM_FLAT)
    lw['ssm_d'] = w['ssm_d'][l][None, :]
    lw['ssm_b_glu'] = w['ssm_b_glu'][l][None, :]
    return lw


def _rope_tables(q_start, t, tm):
    half = ROPE_DIM // 2
    inv = 1.0 / (ROPE_THETA ** (jnp.arange(0, ROPE_DIM, 2, dtype=F32) / ROPE_DIM))
    ang = (q_start + jnp.arange(t)).astype(F32)[:, None] * inv[None, :]
    cos, sin = jnp.cos(ang), jnp.sin(ang)
    z16 = jnp.zeros_like(cos)
    zrest = jnp.zeros((t, LANES - ROPE_DIM), F32)
    tabs = [jnp.tile(cos, (1, MLA_HEADS)), jnp.tile(sin, (1, MLA_HEADS)),
            jnp.concatenate([cos, cos, zrest], axis=1),
            jnp.concatenate([-sin, z16, zrest], axis=1),
            jnp.concatenate([z16, sin, zrest], axis=1)]
    if tm > t:
        tabs = [jnp.tile(a, (tm // t, 1)) for a in tabs]
    return tabs


def _trunk(x, q_start, caches, w, layers, ffn_w):
    bsz, t, d = x.shape
    n = bsz * t
    xf = x.reshape(n, d)
    tm = _pick_tile(n, 256)
    assert t % tm == 0 or tm % t == 0
    tabs = _rope_tables(q_start, t, tm)
    new_rows = []
    for l, lw in enumerate(layers):
        (qcat, kcat, lat, krope, u, qsb0, qsb1, ksb, vsb, gates) = _in_proj(xf, lw, tabs, tm)
        u_tm = u.reshape(bsz, t, SSM_WIDTH).transpose(1, 0, 2).reshape(n, SSM_WIDTH)
        if caches is None:
            olat = _mla_prompt(qcat, kcat, bsz, t)
            oc0, oc1 = _sb_prompt(qsb0, qsb1, ksb, vsb, bsz, t)
            h0_re = jnp.zeros((bsz, SSM_FLAT), F32)
            h0_im = h0_re
        else:
            lat_pool, kr_pool, k_pool, v_pool, s_re, s_im, page_table = caches
            olat = _mla_sample(qcat, kcat, lat_pool, kr_pool, page_table, l, bsz, t)
            oc0, oc1 = _sb_sample(qsb0, qsb1, ksb, vsb, k_pool, v_pool, page_table, l, bsz, t)
            h0_re = s_re[l].reshape(bsz, SSM_FLAT)
            h0_im = s_im[l].reshape(bsz, SSM_FLAT)
        ob_tm, h_re, h_im = _s5(u_tm, h0_re, h0_im, lw, bsz, t)
        ob = ob_tm.reshape(t, bsz, SSM_WIDTH).transpose(1, 0, 2).reshape(n, SSM_WIDTH)
        i = l // 2
        tm_f = _pick_tile(n, 1024)
        if l % 2 == 0:
            x1, hf = _merge(xf, olat, ob, oc0, oc1, gates, lw, None, tm)
            wg, wu, wd = ffn_w['dense'][i]
            xf = _ffn(hf, x1, wg, wu, wd, tm_f, 256)
        else:
            x1, hf, gate = _merge(xf, olat, ob, oc0, oc1, gates, lw, ffn_w['router'][i], tm)
            wg, wu, wd = ffn_w['moe'][i]
            xf = _moe(hf, x1, gate, wg, wu, wd, 512)
        new_rows.append((lat.reshape(bsz, t, KV_RANK), krope.reshape(bsz, t, ROPE_DIM),
                         ksb.reshape(bsz, t, SB_KV_HEADS, SB_HEAD_DIM),
                         vsb.reshape(bsz, t, SB_KV_HEADS, SB_HEAD_DIM),
                         h_re.reshape(bsz, SSM_GROUPS, SSM_STATE),
                         h_im.reshape(bsz, SSM_GROUPS, SSM_STATE)))
    y = _final_norm(xf, w['g_final'][None, :], tm).reshape(bsz, t, d)
    stacked = [jnp.stack([r[j] for r in new_rows]) for j in range(6)]
    return y, stacked


def kernel(x_prompt, x_sample, cache_mla_latent, cache_mla_krope, cache_sb_k, cache_sb_v, state_ssm_re, state_ssm_im, page_table, g_mix, w_in, g_q, g_kv, w_uq, w_uk, w_uv, ssm_lam_re, ssm_lam_im, ssm_log_dt, ssm_b_re, ssm_b_im, ssm_c_re, ssm_c_im, ssm_d, ssm_w_glu, ssm_b_glu, w_br_a, w_br_b, w_br_c, w_out, g_ffn, ffn_w_gate, ffn_w_up, ffn_w_down, moe_router, moe_w_gate, moe_w_up, moe_w_down, g_final):
    w = dict(g_mix=g_mix, w_in=w_in, g_q=g_q, g_kv=g_kv, w_uq=w_uq, w_uk=w_uk, w_uv=w_uv,
             ssm_lam_re=ssm_lam_re, ssm_lam_im=ssm_lam_im, ssm_log_dt=ssm_log_dt,
             ssm_b_re=ssm_b_re, ssm_b_im=ssm_b_im, ssm_c_re=ssm_c_re, ssm_c_im=ssm_c_im,
             ssm_d=ssm_d, ssm_w_glu=ssm_w_glu, ssm_b_glu=ssm_b_glu, w_br_a=w_br_a,
             w_br_b=w_br_b, w_br_c=w_br_c, w_out=w_out, g_ffn=g_ffn, g_final=g_final)
    depth = w_in.shape[0]
    layers = [_prep_layer(w, l) for l in range(depth)]
    d = w_in.shape[1]
    router = jnp.pad(moe_router, ((0, 0), (0, 0), (0, LANES - N_EXPERTS)))
    ffn_w = {
        'dense': [(ffn_w_gate[i].astype(BF), ffn_w_up[i].astype(BF), ffn_w_down[i].astype(BF))
                  for i in range(ffn_w_gate.shape[0])],
        'moe': [(moe_w_gate[i].astype(BF), moe_w_up[i].astype(BF), moe_w_down[i].astype(BF))
                for i in range(moe_w_gate.shape[0])],
        'router': [router[i] for i in range(router.shape[0])],
    }
    past_len = page_table.shape[1] * PAGE_KEYS
    y_p, (lat_p, kr_p, k_p, v_p, re_p, im_p) = _trunk(x_prompt, 0, None, w, layers, ffn_w)
    caches = (cache_mla_latent, cache_mla_krope, cache_sb_k, cache_sb_v,
              state_ssm_re, state_ssm_im, page_table)
    y_s, (lat_s, kr_s, k_s, v_s, re_s, im_s) = _trunk(x_sample, past_len, caches, w, layers, ffn_w)
    return (y_p, y_s, lat_p, lat_s, kr_p, kr_s, k_p, k_s, v_p, v_s, re_p, re_s, im_p, im_s)
```

```python
import functools
import math

import numpy as np
import jax
import jax.numpy as jnp
from jax import lax
from jax.experimental import pallas as pl
from jax.experimental.pallas import tpu as pltpu

F32 = jnp.float32
BF = jnp.bfloat16

MLA_HEADS = 8
QK_NOPE = 64
ROPE_DIM = 32
V_DIM = 64
Q_RANK = 256
KV_RANK = 128
ROPE_THETA = 10000.0
MLA_SCALE = (QK_NOPE + ROPE_DIM) ** -0.5
SSM_WIDTH = 512
SSM_GROUP_SIZE = 16
SSM_GROUPS = SSM_WIDTH // SSM_GROUP_SIZE
SSM_STATE = 64
SSM_FLAT = SSM_GROUPS * SSM_STATE
SB_HEADS = 8
SB_KV_HEADS = 2
SB_HEAD_DIM = 64
SB_GROUP = SB_HEADS // SB_KV_HEADS
SB_SCALE = SB_HEAD_DIM ** -0.5
N_EXPERTS = 8
EPS = 1e-6
NEG_BIG = -1e30
LOG2E = math.log2(math.e)
MLA_Q_SCALE = MLA_SCALE * LOG2E
SB_Q_SCALE = SB_SCALE * LOG2E

LANES = 128
SUBLANES = 8
VMEM_LIMIT_BYTES = 56 * 1024 * 1024

ROPE_PAD = LANES - ROPE_DIM
SEG_CQ = (0, 256)
SEG_CKV = (256, 384)
SEG_KROPE = (384, 512)
SEG_U = (512, 1024)
SEG_QSB = (1024, 1536)
SEG_KSB = (1536, 1664)
SEG_VSB = (1664, 1792)
SEG_GATES = (1792, 4864)
IN_PADDED = 4864

PAGE_KEYS = 128
SAMPLE_PAGES_PER_STEP = 64
SB_KEY_BLOCK = 256
MLA_KEY_BLOCK = 512
MLA_Q_POS_BLOCK = 256
MOE_ROW_TILE = 1024
ROW_COPY_UNROLL = 8
MLA_ROW_SUB = 128
MLA_SCORE_LOOKAHEAD = 4
Q_POS_BLOCK = 256


def _params(*sem):
    return pltpu.CompilerParams(dimension_semantics=sem, vmem_limit_bytes=VMEM_LIMIT_BYTES)


def _pick_tile(n, pref):
    t = min(n, pref)
    while n % t:
        t -= SUBLANES
    return t


def _rms(x, g):
    return x * lax.rsqrt(jnp.mean(x * x, axis=-1, keepdims=True) + EPS) * g


def _dot(a, b):
    return jnp.dot(a, b, preferred_element_type=F32)


def _dot_nt(a, b):
    return lax.dot_general(a, b, (((1,), (1,)), ((), ())), preferred_element_type=F32)


def _const_spec(shape):
    nd = len(shape)
    return pl.BlockSpec(shape, lambda *_: (0,) * nd)


def _in_proj_kernel(x_ref, gmix_ref, win_ref, gq_ref, gkv_ref, wuq_ref, wukbd_ref, perm_ref,
                    cos8_ref, sin8_ref, ka_ref, kb_ref, kc_ref,
                    qcat_ref, kcat_ref, lat_ref, kr_ref, u_ref, qsb0_ref, qsb1_ref,
                    ksb_ref, vsb_ref, gates_ref):
    h = _rms(x_ref[...], gmix_ref[...]).astype(BF)

    def seg(bounds):
        return _dot(h, win_ref[:, bounds[0]:bounds[1]])

    qn = _rms(seg(SEG_CQ), gq_ref[...]).astype(BF)
    q = _dot(qn, wuq_ref[...])
    n_nope = MLA_HEADS * QK_NOPE
    x1 = q[:, n_nope:n_nope + LANES]
    x2 = q[:, n_nope + LANES:n_nope + 2 * LANES]
    c8 = cos8_ref[...]
    s8 = sin8_ref[...]
    rot = (jnp.concatenate([x1 * c8 - x2 * s8, x1 * s8 + x2 * c8], axis=1) * MLA_Q_SCALE).astype(BF)
    qr = _dot(rot, perm_ref[...])
    for pair in range(MLA_HEADS // 2):
        qn_pair = q[:, LANES * pair:LANES * (pair + 1)].astype(BF)
        ql = _dot(qn_pair, wukbd_ref[pair]) * MLA_Q_SCALE
        for j in range(2):
            hd = 2 * pair + j
            qcat_ref[:, 256 * hd:256 * hd + LANES] = ql[:, LANES * j:LANES * (j + 1)].astype(BF)
            qcat_ref[:, 256 * hd + LANES:256 * (hd + 1)] = qr[:, LANES * hd:LANES * (hd + 1)].astype(BF)

    lat = _rms(seg(SEG_CKV), gkv_ref[...])
    lat_ref[...] = lat
    zk = seg(SEG_KROPE)
    half = ROPE_DIM // 2
    kr = (zk * ka_ref[...] + pltpu.roll(zk, LANES - half, 1) * kb_ref[...]
          + pltpu.roll(zk, half, 1) * kc_ref[...])
    kr_ref[...] = kr[:, :ROPE_DIM]
    kcat_ref[:, :LANES] = lat.astype(BF)
    kcat_ref[:, LANES:] = kr.astype(BF)

    u_ref[...] = seg(SEG_U)
    qs = seg(SEG_QSB) * SB_Q_SCALE
    hw = SB_GROUP * SB_HEAD_DIM
    qsb0_ref[...] = qs[:, :hw].astype(BF)
    qsb1_ref[...] = qs[:, hw:].astype(BF)
    ksb_ref[...] = seg(SEG_KSB)
    vsb_ref[...] = seg(SEG_VSB)
    g0 = SEG_GATES[0]
    d = (SEG_GATES[1] - g0) // 3
    for j in range(3):
        gates_ref[:, d * j:d * (j + 1)] = jax.nn.sigmoid(seg((g0 + d * j, g0 + d * (j + 1)))).astype(BF)


def _time_major_spec(tm, t):
    nt = t // tm
    return pl.BlockSpec((tm, SSM_WIDTH), lambda i: (i % nt, i // nt))


def _in_proj(x, lw, tabs, tm, bsz, t):
    n, d = x.shape
    nblk = tabs[0].shape[0] // tm
    row = lambda w: pl.BlockSpec((tm, w), lambda i: (i, 0))
    tab = pl.BlockSpec((tm, LANES), lambda i: (i % nblk, 0))
    hw = SB_GROUP * SB_HEAD_DIM
    kvw = SB_KV_HEADS * SB_HEAD_DIM
    u_time_major = t % tm == 0
    out_shape = (
        jax.ShapeDtypeStruct((n, MLA_HEADS * 256), BF),
        jax.ShapeDtypeStruct((n, 256), BF),
        jax.ShapeDtypeStruct((n, KV_RANK), F32),
        jax.ShapeDtypeStruct((n, ROPE_DIM), F32),
        jax.ShapeDtypeStruct((t, bsz * SSM_WIDTH) if u_time_major else (n, SSM_WIDTH), F32),
        jax.ShapeDtypeStruct((n, hw), BF),
        jax.ShapeDtypeStruct((n, hw), BF),
        jax.ShapeDtypeStruct((n, kvw), F32),
        jax.ShapeDtypeStruct((n, kvw), F32),
        jax.ShapeDtypeStruct((n, 3 * d), BF),
    )
    out_specs = [row(s.shape[1]) for s in out_shape]
    if u_time_major:
        out_specs[4] = _time_major_spec(tm, t)
    out_specs = tuple(out_specs)
    consts = [lw['g_mix'], lw['w_in_p'], lw['g_q'], lw['g_kv'], lw['w_uq_p'], lw['w_uk_bd'], lw['perm']]
    return pl.pallas_call(
        _in_proj_kernel,
        grid=(n // tm,),
        in_specs=[row(d)] + [_const_spec(c.shape) for c in consts] + [tab] * 5,
        out_specs=out_specs,
        out_shape=out_shape,
        compiler_params=_params("parallel"),
        name="in_proj",
    )(x, *consts, *tabs)


def _mla_prompt_kernel(qi_ref, kj_ref, last_ref, q_ref, k_ref, o_ref, m_sc, acc_sc, *, tq, kb, rs):
    p = pl.program_id(1)
    qi = qi_ref[p]
    kj = kj_ref[p]
    rows = tq * MLA_HEADS

    @pl.when(kj == 0)
    def _():
        m_sc[...] = jnp.full(m_sc.shape, NEG_BIG, F32)
        acc_sc[...] = jnp.zeros(acc_sc.shape, F32)

    k = k_ref[...]
    v_ext = jnp.concatenate([k[:, :KV_RANK], jnp.ones((kb, LANES), BF)], axis=1)

    def process(masked):
        n_sub = rows // rs
        pending = []
        for step in range(n_sub + MLA_SCORE_LOOKAHEAD):
            if step < n_sub:
                pending.append(_dot_nt(q_ref[step * rs:(step + 1) * rs, :], k))
            r = step - MLA_SCORE_LOOKAHEAD
            if r < 0:
                continue
            sl = slice(r * rs, (r + 1) * rs)
            s = pending.pop(0)
            if masked:
                row = lax.broadcasted_iota(jnp.int32, s.shape, 0)
                col = lax.broadcasted_iota(jnp.int32, s.shape, 1)
                q_pos = qi * tq + (r * rs + row) // MLA_HEADS
                s = jnp.where(kj * kb + col <= q_pos, s, NEG_BIG)
            m_prev = m_sc[sl, :]
            m_new = jnp.maximum(m_prev, jnp.max(s, axis=1, keepdims=True))
            alpha = jnp.exp2(m_prev - m_new)
            pr = jnp.exp2(s - m_new).astype(BF)
            acc_sc[sl, :] = alpha * acc_sc[sl, :] + _dot(pr, v_ext)
            m_sc[sl, :] = m_new

    needs_mask = kj * kb + kb - 1 > qi * tq

    @pl.when(needs_mask)
    def _():
        process(True)

    @pl.when(jnp.logical_not(needs_mask))
    def _():
        process(False)

    @pl.when(last_ref[p] == 1)
    def _():
        acc = acc_sc[...]
        o_ref[...] = (acc[:, :KV_RANK] / acc[:, KV_RANK:KV_RANK + 1]).astype(BF)


def _causal_pairs(t, tq, kb, strict, descending):
    qi_l, kj_l, first_l, last_l = [], [], [], []
    for qi in range(t // tq):
        last_key = qi * tq + tq - 1 - (1 if strict else 0)
        ks = list(range(last_key // kb + 1))
        if descending:
            ks = ks[::-1]
        for n, kj in enumerate(ks):
            qi_l.append(qi)
            kj_l.append(kj)
            first_l.append(1 if n == 0 else 0)
            last_l.append(1 if n == len(ks) - 1 else 0)
    as_arr = lambda v: jnp.asarray(np.asarray(v, np.int32))
    return as_arr(qi_l), as_arr(kj_l), as_arr(first_l), as_arr(last_l)


def _mla_prompt(qcat, kcat, bsz, t):
    tq = min(MLA_Q_POS_BLOCK, t)
    kb = min(MLA_KEY_BLOCK, t)
    qi_t, kj_t, _, last_t = _causal_pairs(t, tq, kb, strict=False, descending=False)
    rows = tq * MLA_HEADS
    rs = min(MLA_ROW_SUB, rows)
    q2 = qcat.reshape(bsz * t * MLA_HEADS, 256)
    nq, nk = t // tq, t // kb
    grid_spec = pltpu.PrefetchScalarGridSpec(
        num_scalar_prefetch=3,
        grid=(bsz, int(qi_t.shape[0])),
        in_specs=[
            pl.BlockSpec((rows, 256), lambda b, p, qi, kj, la: (b * nq + qi[p], 0)),
            pl.BlockSpec((kb, 256), lambda b, p, qi, kj, la: (b * nk + kj[p], 0)),
        ],
        out_specs=pl.BlockSpec((rows, KV_RANK), lambda b, p, qi, kj, la: (b * nq + qi[p], 0)),
        scratch_shapes=[pltpu.VMEM((rows, 1), F32), pltpu.VMEM((rows, 2 * LANES), F32)],
    )
    o = pl.pallas_call(
        functools.partial(_mla_prompt_kernel, tq=tq, kb=kb, rs=rs),
        grid_spec=grid_spec,
        out_shape=jax.ShapeDtypeStruct((bsz * t * MLA_HEADS, KV_RANK), BF),
        compiler_params=_params("parallel", "arbitrary"),
        name="mla_prompt",
    )(qi_t, kj_t, last_t, q2, kcat)
    return o.reshape(bsz * t, MLA_HEADS * KV_RANK)


def _stream_page_chunks(pt_ref, pools, bufs, sems, chunk_of_step, layer, pb, page_axis, compute):
    b = pl.program_id(0)
    j = pl.program_id(1)
    nb = pl.num_programs(0)
    nch = pl.num_programs(1) - 1

    def copies(bb, jj, slot):
        chunk = chunk_of_step(jj)
        out = []
        for i in range(pb):
            page = pt_ref[bb, chunk * pb + i]
            for a, (pool, buf) in enumerate(zip(pools, bufs)):
                extent = pool.shape[2 + page_axis[a]]
                window = pl.ds(i * extent, extent)
                dst = buf.at[slot, window, :] if page_axis[a] == 0 else buf.at[slot, :, window]
                out.append(pltpu.make_async_copy(pool.at[layer, page], dst, sems.at[a, slot]))
        return out

    @pl.when((j == 0) & (b == 0))
    def _():
        for cp in copies(0, 1, 0):
            cp.start()

    @pl.when(j > 0)
    def _():
        slot = (b * nch + j - 1) % 2
        for cp in copies(b, j, slot):
            cp.wait()

        @pl.when(j < nch)
        def _():
            for cp in copies(b, j + 1, 1 - slot):
                cp.start()

        @pl.when((j == nch) & (b + 1 < nb))
        def _():
            for cp in copies(b + 1, 1, 1 - slot):
                cp.start()

        compute(slot)


def _mla_sample_kernel(pt_ref, q_ref, newk_ref, lat_hbm, kr_hbm, o_ref, lat_buf, kr_buf, sems,
                       m_sc, l_sc, acc_sc, *, pb, layer):
    j = pl.program_id(1)
    q = q_ref[...]

    def update(s, vals):
        m_prev = m_sc[...]
        m_new = jnp.maximum(m_prev, jnp.max(s, axis=1, keepdims=True))
        alpha = jnp.exp2(m_prev - m_new)
        pr = jnp.exp2(s - m_new)
        l_sc[...] = alpha * l_sc[...] + jnp.sum(pr, axis=1, keepdims=True)
        acc_sc[...] = alpha * acc_sc[...] + _dot(pr.astype(BF), vals)
        m_sc[...] = m_new

    @pl.when(j == 0)
    def _():
        m_sc[...] = jnp.full(m_sc.shape, NEG_BIG, F32)
        l_sc[...] = jnp.zeros(l_sc.shape, F32)
        acc_sc[...] = jnp.zeros(acc_sc.shape, F32)
        nk = newk_ref[...]
        s = _dot_nt(q, nk)
        row = lax.broadcasted_iota(jnp.int32, s.shape, 0)
        col = lax.broadcasted_iota(jnp.int32, s.shape, 1)
        s = jnp.where(col <= row // MLA_HEADS, s, NEG_BIG)
        update(s, nk[:, :KV_RANK])

    def paged(slot):
        q_lat = q[:, :KV_RANK]
        q_rope = q[:, KV_RANK:KV_RANK + ROPE_DIM]
        lat = lat_buf[slot].astype(BF)
        kr_t = kr_buf[slot].astype(BF)
        update(_dot_nt(q_lat, lat) + _dot(q_rope, kr_t), lat)

    _stream_page_chunks(pt_ref, (lat_hbm, kr_hbm), (lat_buf, kr_buf), sems, lambda jj: jj - 1,
                        layer, pb, (0, 1), paged)

    @pl.when(j == pl.num_programs(1) - 1)
    def _():
        o_ref[...] = (acc_sc[...] / l_sc[...]).astype(BF)


def _mla_sample(qcat, kcat, lat_pool, kr_pool, page_table, layer, bsz, t):
    n_pages = page_table.shape[1]
    pb = min(SAMPLE_PAGES_PER_STEP, n_pages)
    nch = n_pages // pb
    rows = t * MLA_HEADS
    q3 = qcat.reshape(bsz, rows, 256)
    newk = jnp.pad(kcat.reshape(bsz, t, 256), ((0, 0), (0, PAGE_KEYS - t), (0, 0)))
    kr_pool = jnp.swapaxes(kr_pool, 2, 3)
    hbm = pl.BlockSpec(memory_space=pl.ANY)
    grid_spec = pltpu.PrefetchScalarGridSpec(
        num_scalar_prefetch=1,
        grid=(bsz, nch + 1),
        in_specs=[pl.BlockSpec((None, rows, 256), lambda b, j, pt: (b, 0, 0)),
                  pl.BlockSpec((None, PAGE_KEYS, 256), lambda b, j, pt: (b, 0, 0)),
                  hbm, hbm],
        out_specs=pl.BlockSpec((None, rows, KV_RANK), lambda b, j, pt: (b, 0, 0)),
        scratch_shapes=[pltpu.VMEM((2, pb * PAGE_KEYS, KV_RANK), F32),
                        pltpu.VMEM((2, ROPE_DIM, pb * PAGE_KEYS), F32),
                        pltpu.SemaphoreType.DMA((2, 2)),
                        pltpu.VMEM((rows, 1), F32), pltpu.VMEM((rows, 1), F32),
                        pltpu.VMEM((rows, KV_RANK), F32)],
    )
    o = pl.pallas_call(
        functools.partial(_mla_sample_kernel, pb=pb, layer=layer),
        grid_spec=grid_spec,
        out_shape=jax.ShapeDtypeStruct((bsz, rows, KV_RANK), BF),
        compiler_params=_params("arbitrary", "arbitrary"),
        name="mla_sample",
    )(page_table, q3, newk, lat_pool, kr_pool)
    return o.reshape(bsz * t, MLA_HEADS * KV_RANK)


def _sb_log2_1m(z, mask):
    log_1m = -(jnp.maximum(z, 0.0) + jnp.log2(1.0 + jnp.exp2(-jnp.abs(z))))
    if mask is not None:
        log_1m = jnp.where(mask, log_1m, 0.0)
    return log_1m


def _suffix_sums(log_1m, tri, two_terms=True):
    hi = log_1m.astype(BF)
    if not two_terms:
        return _dot(hi, tri)
    lo = (log_1m - hi.astype(F32)).astype(BF)
    return _dot(hi, tri) + _dot(lo, tri)


def _sb_weights(z, log_1m, between, mask):
    a = jnp.exp2(z + log_1m + between)
    if mask is not None:
        a = jnp.where(mask, a, 0.0)
    return a.astype(BF)


def _sb_prompt_kernel(qi_ref, kj_ref, first_ref, last_ref, q0_ref, q1_ref, k_ref, v_ref, tri_ref,
                      o0_ref, o1_ref, carry_sc, acc_sc, *, tq, kb):
    p = pl.program_id(1)
    qi = qi_ref[p]
    kj = kj_ref[p]

    @pl.when(first_ref[p] == 1)
    def _():
        carry_sc[...] = jnp.zeros(carry_sc.shape, F32)
        acc_sc[...] = jnp.zeros(acc_sc.shape, F32)

    rows = tq * SB_GROUP

    def process(masked):
        mask = None
        if masked:
            row = lax.broadcasted_iota(jnp.int32, (rows, kb), 0)
            col = lax.broadcasted_iota(jnp.int32, (rows, kb), 1)
            mask = (kj * kb + col) < (qi * tq + row // SB_GROUP)
        tri = tri_ref[...]
        kf = k_ref[...]
        vf = v_ref[...]
        for kv, q_ref in enumerate((q0_ref, q1_ref)):
            k = kf[:, SB_HEAD_DIM * kv:SB_HEAD_DIM * (kv + 1)].astype(BF)
            v = vf[:, SB_HEAD_DIM * kv:SB_HEAD_DIM * (kv + 1)].astype(BF)
            z = _dot_nt(q_ref[...], k)
            log_1m = _sb_log2_1m(z, mask)
            w = _suffix_sums(log_1m, tri, two_terms=False)
            carry = carry_sc[kv]
            a = _sb_weights(z, log_1m, carry + w, mask)
            acc_sc[kv] = acc_sc[kv] + _dot(a, v)
            carry_sc[kv] = carry + (w + log_1m)[:, 0:1]

    needs_mask = kj * kb + kb - 1 >= qi * tq

    @pl.when(needs_mask)
    def _():
        process(True)

    @pl.when(jnp.logical_not(needs_mask))
    def _():
        process(False)

    @pl.when(last_ref[p] == 1)
    def _():
        o0_ref[...] = acc_sc[0].astype(BF)
        o1_ref[...] = acc_sc[1].astype(BF)


def _strict_lower_ones(n):
    j = np.arange(n)[:, None]
    s = np.arange(n)[None, :]
    return jnp.asarray((j > s).astype(np.float32), BF)


def _sb_prompt(qsb0, qsb1, ksb, vsb, bsz, t):
    tq = min(Q_POS_BLOCK, t)
    kb = min(SB_KEY_BLOCK, t)
    qi_t, kj_t, first_t, last_t = _causal_pairs(t, tq, kb, strict=True, descending=True)
    rows = tq * SB_GROUP
    nq, nk = t // tq, t // kb
    n = bsz * t
    q0 = qsb0.reshape(n * SB_GROUP, SB_HEAD_DIM)
    q1 = qsb1.reshape(n * SB_GROUP, SB_HEAD_DIM)
    kvw = SB_KV_HEADS * SB_HEAD_DIM
    qspec = pl.BlockSpec((rows, SB_HEAD_DIM), lambda b, p, qi, kj, fi, la: (b * nq + qi[p], 0))
    kspec = pl.BlockSpec((kb, kvw), lambda b, p, qi, kj, fi, la: (b * nk + kj[p], 0))
    grid_spec = pltpu.PrefetchScalarGridSpec(
        num_scalar_prefetch=4,
        grid=(bsz, int(qi_t.shape[0])),
        in_specs=[qspec, qspec, kspec, kspec,
                  pl.BlockSpec((kb, kb), lambda b, p, qi, kj, fi, la: (0, 0))],
        out_specs=(qspec, qspec),
        scratch_shapes=[pltpu.VMEM((SB_KV_HEADS, rows, 1), F32),
                        pltpu.VMEM((SB_KV_HEADS, rows, SB_HEAD_DIM), F32)],
    )
    o_sds = jax.ShapeDtypeStruct((n * SB_GROUP, SB_HEAD_DIM), BF)
    o0, o1 = pl.pallas_call(
        functools.partial(_sb_prompt_kernel, tq=tq, kb=kb),
        grid_spec=grid_spec,
        out_shape=(o_sds, o_sds),
        compiler_params=_params("parallel", "arbitrary"),
        name="sb_prompt",
    )(qi_t, kj_t, first_t, last_t, q0, q1, ksb, vsb, _strict_lower_ones(kb))
    hw = SB_GROUP * SB_HEAD_DIM
    return o0.reshape(n, hw), o1.reshape(n, hw)


def _sb_sample_kernel(pt_ref, q_ref, newk_ref, newv_ref, tri_new_ref, tri_ref, k_hbm, v_hbm,
                      o_ref, k_buf, v_buf, sems, carry_sc, acc_sc, *, pb, t_new, layer):
    j = pl.program_id(1)
    q = q_ref[...]
    rows = q.shape[0]
    rows_per_kv = t_new * SB_GROUP

    @pl.when(j == 0)
    def _():
        nk_t = newk_ref[...].astype(BF)
        nv_t = newv_ref[...].astype(BF)
        row = lax.broadcasted_iota(jnp.int32, (rows, PAGE_KEYS), 0)
        col = lax.broadcasted_iota(jnp.int32, (rows, PAGE_KEYS), 1)
        mask = col < (row % rows_per_kv) // SB_GROUP
        z = _dot(q, nk_t)
        log_1m = _sb_log2_1m(z, mask)
        w = _suffix_sums(log_1m, tri_new_ref[...])
        a = _sb_weights(z, log_1m, w, mask)
        acc_sc[...] = _dot_nt(a, nv_t)
        carry_sc[...] = (w + log_1m)[:, 0:1]

    def paged(slot):
        k_t = k_buf[slot].astype(BF)
        v_t = v_buf[slot].astype(BF)
        z = _dot(q, k_t)
        log_1m = _sb_log2_1m(z, None)
        sub = tri_ref.shape[0]
        nsub = z.shape[1] // sub
        stacked = jnp.concatenate([log_1m[:, sub * s:sub * (s + 1)] for s in range(nsub)], axis=0)
        w = _suffix_sums(stacked, tri_ref[...])
        totals = (w + stacked)[:, 0:1]
        carry = carry_sc[...]
        between = [None] * nsub
        for s in reversed(range(nsub)):
            between[s] = carry + w[rows * s:rows * (s + 1), :]
            carry = carry + totals[rows * s:rows * (s + 1), :]
        a = _sb_weights(z, log_1m, jnp.concatenate(between, axis=1), None)
        acc_sc[...] = acc_sc[...] + _dot_nt(a, v_t)
        carry_sc[...] = carry

    _stream_page_chunks(pt_ref, (k_hbm, v_hbm), (k_buf, v_buf), sems,
                        lambda jj: pl.num_programs(1) - 1 - jj, layer, pb, (1, 1), paged)

    @pl.when(j == pl.num_programs(1) - 1)
    def _():
        o_ref[...] = acc_sc[...]


def _sb_sample(qsb0, qsb1, ksb, vsb, k_pool, v_pool, page_table, layer, bsz, t):
    n_pages = page_table.shape[1]
    pb = min(SAMPLE_PAGES_PER_STEP, n_pages)
    nch = n_pages // pb
    kvw = SB_KV_HEADS * SB_HEAD_DIM
    rows = SB_KV_HEADS * t * SB_GROUP
    sub = min(SB_KEY_BLOCK, pb * PAGE_KEYS)
    q0 = qsb0.reshape(bsz, t * SB_GROUP, SB_HEAD_DIM)
    q1 = qsb1.reshape(bsz, t * SB_GROUP, SB_HEAD_DIM)
    zq = jnp.zeros_like(q0)
    q = jnp.concatenate([jnp.concatenate([q0, zq], axis=2), jnp.concatenate([zq, q1], axis=2)], axis=1)
    pad = ((0, 0), (0, 0), (0, PAGE_KEYS - t))
    newk = jnp.pad(jnp.swapaxes(ksb.reshape(bsz, t, kvw), 1, 2), pad)
    newv = jnp.pad(jnp.swapaxes(vsb.reshape(bsz, t, kvw), 1, 2), pad)
    depth, n_phys = k_pool.shape[0], k_pool.shape[1]
    kp = jnp.transpose(k_pool, (0, 1, 3, 4, 2)).reshape(depth, n_phys, kvw, PAGE_KEYS)
    vp = jnp.transpose(v_pool, (0, 1, 3, 4, 2)).reshape(depth, n_phys, kvw, PAGE_KEYS)

    per_b = lambda r, w: pl.BlockSpec((None, r, w), lambda b, j, pt: (b, 0, 0))
    const2 = lambda r: pl.BlockSpec((r, r), lambda b, j, pt: (0, 0))
    hbm = pl.BlockSpec(memory_space=pl.ANY)
    grid_spec = pltpu.PrefetchScalarGridSpec(
        num_scalar_prefetch=1,
        grid=(bsz, nch + 1),
        in_specs=[per_b(rows, kvw), per_b(kvw, PAGE_KEYS), per_b(kvw, PAGE_KEYS),
                  const2(PAGE_KEYS), const2(sub), hbm, hbm],
        out_specs=per_b(rows, kvw),
        scratch_shapes=[pltpu.VMEM((2, kvw, pb * PAGE_KEYS), F32),
                        pltpu.VMEM((2, kvw, pb * PAGE_KEYS), F32),
                        pltpu.SemaphoreType.DMA((2, 2)),
                        pltpu.VMEM((rows, 1), F32), pltpu.VMEM((rows, kvw), F32)],
    )
    o = pl.pallas_call(
        functools.partial(_sb_sample_kernel, pb=pb, t_new=t, layer=layer),
        grid_spec=grid_spec,
        out_shape=jax.ShapeDtypeStruct((bsz, rows, kvw), F32),
        compiler_params=_params("arbitrary", "arbitrary"),
        name="sb_sample",
    )(page_table, q, newk, newv, _strict_lower_ones(PAGE_KEYS), _strict_lower_ones(sub), kp, vp)
    o = o.reshape(bsz, SB_KV_HEADS, t * SB_GROUP, SB_KV_HEADS, SB_HEAD_DIM)
    hw = SB_GROUP * SB_HEAD_DIM
    o0 = o[:, 0, :, 0, :].reshape(bsz * t, hw).astype(BF)
    o1 = o[:, 1, :, 1, :].reshape(bsz * t, hw).astype(BF)
    return o0, o1


def _s5_kernel(u_ref, h0re_ref, h0im_ref, are_ref, aim_ref, bbd_ref, cbd_ref, dskip_ref, wglu_ref,
               bglu_ref, ob_ref, hre_ref, him_ref, s_sc, st_sc, *, bn, tc, lc):
    c = pl.program_id(0)

    @pl.when(c == 0)
    def _():
        st_sc[0] = h0re_ref[...]
        st_sc[1] = h0im_ref[...]

    u = u_ref[...]
    ub = u.astype(BF)
    nflat = SSM_FLAT
    for jb in range(2 * nflat // 512):
        s_sc[:, 512 * jb:512 * (jb + 1)] = _dot(ub, bbd_ref[:, 512 * jb:512 * (jb + 1)])

    for ci in range(nflat // lc):
        lo = ci * lc
        ar = jnp.broadcast_to(are_ref[:, lo:lo + lc], (bn, lc))
        ai = jnp.broadcast_to(aim_ref[:, lo:lo + lc], (bn, lc))

        def body(t, carry, lo=lo, ar=ar, ai=ai):
            hr, hi = carry
            r0 = pl.multiple_of(t * bn, bn)
            br = s_sc[pl.ds(r0, bn), lo:lo + lc]
            bi = s_sc[pl.ds(r0, bn), nflat + lo:nflat + lo + lc]
            nr = ar * hr - ai * hi + br
            ni = ar * hi + ai * hr + bi
            s_sc[pl.ds(r0, bn), lo:lo + lc] = nr
            s_sc[pl.ds(r0, bn), nflat + lo:nflat + lo + lc] = ni
            return nr, ni

        hr, hi = lax.fori_loop(0, tc, body, (st_sc[0, :, lo:lo + lc], st_sc[1, :, lo:lo + lc]))
        st_sc[0, :, lo:lo + lc] = hr
        st_sc[1, :, lo:lo + lc] = hi

    y = dskip_ref[...] * u
    for jb in range(2 * nflat // 1024):
        y = y + _dot(s_sc[:, 1024 * jb:1024 * (jb + 1)].astype(BF), cbd_ref[1024 * jb:1024 * (jb + 1), :])
    g = jax.nn.gelu(y)
    gate = jax.nn.sigmoid(_dot(g.astype(BF), wglu_ref[...]) + bglu_ref[...])
    ob_ref[...] = (g * gate).astype(BF)

    @pl.when(c == pl.num_programs(0) - 1)
    def _():
        hre_ref[...] = st_sc[0]
        him_ref[...] = st_sc[1]


def _s5(u_tm, h0_re, h0_im, lw, bn, t):
    tc = t
    while tc * bn > 512 and tc % 2 == 0:
        tc //= 2
    rows = tc * bn
    lc = 256 if bn <= 16 else LANES
    nflat = SSM_FLAT
    consts = [lw['ssm_a_re'], lw['ssm_a_im'], lw['ssm_bbd'], lw['ssm_cbd'], lw['ssm_d'],
              lw['ssm_w_glu'], lw['ssm_b_glu']]
    st_spec = pl.BlockSpec((bn, nflat), lambda c: (0, 0))
    st_sds = jax.ShapeDtypeStruct((bn, nflat), F32)
    return pl.pallas_call(
        functools.partial(_s5_kernel, bn=bn, tc=tc, lc=lc),
        grid=(t // tc,),
        in_specs=[pl.BlockSpec((rows, SSM_WIDTH), lambda c: (c, 0)), st_spec, st_spec]
                 + [_const_spec(a.shape) for a in consts],
        out_specs=(pl.BlockSpec((rows, SSM_WIDTH), lambda c: (c, 0)), st_spec, st_spec),
        out_shape=(jax.ShapeDtypeStruct((t * bn, SSM_WIDTH), BF), st_sds, st_sds),
        scratch_shapes=[pltpu.VMEM((rows, 2 * nflat), F32), pltpu.VMEM((2, bn, nflat), F32)],
        compiler_params=_params("arbitrary"),
        name="s5",
    )(u_tm, h0_re, h0_im, *consts)


def _merge_kernel(x_ref, olat_ref, ob_ref, oc0_ref, oc1_ref, gates_ref, wuv_ref, wa_ref, wb_ref,
                  wc_ref, wout_ref, gffn_ref, *rest, routed):
    if routed:
        router_ref, x1_ref, hf_ref, route_ref = rest
    else:
        x1_ref, hf_ref = rest
    d = x_ref.shape[1]
    hw = SB_GROUP * SB_HEAD_DIM
    o_a = _dot(olat_ref[...], wuv_ref[...]).astype(BF)
    br_a = _dot(o_a, wa_ref[...])
    br_b = _dot(ob_ref[...], wb_ref[...])
    br_c = _dot(oc0_ref[...], wc_ref[:hw, :]) + _dot(oc1_ref[...], wc_ref[hw:, :])
    merged = (gates_ref[:, :d].astype(F32) * br_a + gates_ref[:, d:2 * d].astype(F32) * br_b
              + gates_ref[:, 2 * d:].astype(F32) * br_c)
    x1 = x_ref[...] + _dot(merged.astype(BF), wout_ref[...])
    x1_ref[...] = x1
    hf = _rms(x1, gffn_ref[...])
    hf_ref[...] = hf.astype(hf_ref.dtype)
    if routed:
        router = router_ref[...]
        r_hi = router.astype(BF)
        r_lo = (router - r_hi.astype(F32)).astype(BF)
        h_hi = hf.astype(BF)
        h_lo = (hf - h_hi.astype(F32)).astype(BF)
        logits = _dot(h_hi, r_hi) + (_dot(h_hi, r_lo) + _dot(h_lo, r_hi))
        lane = lax.broadcasted_iota(jnp.int32, logits.shape, 1)
        lg = jnp.where(lane < N_EXPERTS, logits, NEG_BIG)
        v1 = jnp.max(lg, axis=1, keepdims=True)
        i1 = jnp.min(jnp.where(lg == v1, lane, LANES), axis=1, keepdims=True)
        lg2 = jnp.where(lane == i1, NEG_BIG, lg)
        v2 = jnp.max(lg2, axis=1, keepdims=True)
        i2 = jnp.min(jnp.where(lg2 == v2, lane, LANES), axis=1, keepdims=True)
        e2 = jnp.exp(v2 - v1)
        p1 = 1.0 / (1.0 + e2)
        p2 = e2 / (1.0 + e2)
        route_ref[...] = (jnp.where(lane == 0, i1.astype(F32), 0.0) + jnp.where(lane == 1, i2.astype(F32), 0.0)
                          + jnp.where(lane == 2, p1, 0.0) + jnp.where(lane == 3, p2, 0.0))


def _merge(x, olat, ob, oc0, oc1, gates, lw, router, tm, t):
    n, d = x.shape
    routed = router is not None
    row = lambda w: pl.BlockSpec((tm, w), lambda i: (i, 0))
    consts = [lw['w_uv_bd'], lw['w_br_a'], lw['w_br_b'], lw['w_br_c'], lw['w_out'], lw['g_ffn']]
    if routed:
        consts.append(router)
    acts = [x, olat, ob, oc0, oc1, gates]
    act_specs = [row(a.shape[1]) for a in acts]
    if ob.shape[0] != n:
        act_specs[2] = _time_major_spec(tm, t)
    out_shape = [jax.ShapeDtypeStruct((n, d), F32), jax.ShapeDtypeStruct((n, d), F32 if routed else BF)]
    if routed:
        out_shape.append(jax.ShapeDtypeStruct((n, LANES), F32))
    return pl.pallas_call(
        functools.partial(_merge_kernel, routed=routed),
        grid=(n // tm,),
        in_specs=act_specs + [_const_spec(c.shape) for c in consts],
        out_specs=tuple(row(s.shape[1]) for s in out_shape),
        out_shape=tuple(out_shape),
        compiler_params=_params("parallel"),
        name="merge",
    )(*acts, *consts)


def _ffn_kernel(h_ref, x_ref, wg_ref, wu_ref, wd_ref, o_ref, acc_sc):
    f = pl.program_id(1)

    @pl.when(f == 0)
    def _():
        acc_sc[...] = jnp.zeros(acc_sc.shape, F32)

    h = h_ref[...]
    act = (jax.nn.silu(_dot(h, wg_ref[...])) * _dot(h, wu_ref[...])).astype(BF)
    acc_sc[...] += _dot(act, wd_ref[...])

    @pl.when(f == pl.num_programs(1) - 1)
    def _():
        o_ref[...] = x_ref[...] + acc_sc[...]


def _ffn(hf, x, wg, wu, wd, tm, tf):
    n, d = x.shape
    dff = wg.shape[1]
    return pl.pallas_call(
        _ffn_kernel,
        grid=(n // tm, dff // tf),
        in_specs=[pl.BlockSpec((tm, d), lambda i, f: (i, 0)),
                  pl.BlockSpec((tm, d), lambda i, f: (i, 0)),
                  pl.BlockSpec((d, tf), lambda i, f: (0, f)),
                  pl.BlockSpec((d, tf), lambda i, f: (0, f)),
                  pl.BlockSpec((tf, d), lambda i, f: (f, 0))],
        out_specs=pl.BlockSpec((tm, d), lambda i, f: (i, 0)),
        out_shape=jax.ShapeDtypeStruct((n, d), F32),
        scratch_shapes=[pltpu.VMEM((tm, d), F32)],
        compiler_params=_params("parallel", "arbitrary"),
        name="ffn_dense",
    )(hf, x, wg, wu, wd)


def _route_plan(route, tr):
    n = route.shape[0]
    experts = route[:, :2].astype(jnp.int32)
    hot = (experts[:, :, None] == jnp.arange(N_EXPERTS, dtype=jnp.int32)[None, None, :]).astype(jnp.int32).sum(1)
    padded = ((hot.sum(0) + tr - 1) // tr) * tr
    ends = jnp.cumsum(padded)
    dest = (ends - padded)[None, :] + jnp.cumsum(hot, axis=0) - hot
    pos = jnp.take_along_axis(dest, experts, axis=1).astype(jnp.int32)
    n_tiles = (2 * n) // tr + N_EXPERTS
    starts = jnp.arange(n_tiles, dtype=jnp.int32) * tr
    tile_expert = jnp.minimum(jnp.searchsorted(ends, starts, side='right'), N_EXPERTS - 1).astype(jnp.int32)
    n_used = (ends[-1] // tr).astype(jnp.int32).reshape(1)
    return pos[:, 0], pos[:, 1], tile_expert, n_used, n_tiles


def _row_copy_loop(n_rows, make_copies, wait):
    def body(t, carry):
        for cp in make_copies(t):
            if wait:
                cp.wait()
            else:
                cp.start()
        return carry

    lax.fori_loop(0, n_rows, body, 0, unroll=ROW_COPY_UNROLL)


def _dispatch_kernel(pos1_ref, pos2_ref, h_ref, xs_init_ref, xs_ref, sem, *, tm):
    del xs_init_ref
    base = pl.program_id(0) * tm

    def copies(t):
        src = h_ref.at[pl.ds(t, 1), :]
        return [pltpu.make_async_copy(src, xs_ref.at[pl.ds(pos_ref[base + t], 1), :], sem)
                for pos_ref in (pos1_ref, pos2_ref)]

    _row_copy_loop(tm, copies, wait=False)
    _row_copy_loop(tm, copies, wait=True)


def _dispatch(hf, pos1, pos2, n_rows, tm):
    n, d = hf.shape
    grid_spec = pltpu.PrefetchScalarGridSpec(
        num_scalar_prefetch=2,
        grid=(n // tm,),
        in_specs=[pl.BlockSpec((tm, d), lambda i, p1, p2: (i, 0)), pl.BlockSpec(memory_space=pl.ANY)],
        out_specs=pl.BlockSpec(memory_space=pl.ANY),
        scratch_shapes=[pltpu.SemaphoreType.DMA(())],
    )
    return pl.pallas_call(
        functools.partial(_dispatch_kernel, tm=tm),
        grid_spec=grid_spec,
        out_shape=jax.ShapeDtypeStruct((n_rows, d), F32),
        input_output_aliases={3: 0},
        compiler_params=_params("arbitrary"),
        name="moe_dispatch",
    )(pos1, pos2, hf, jnp.zeros((n_rows, d), F32))


def _expert_ffn_kernel(te_ref, nu_ref, xs_ref, wg_ref, wu_ref, wd_ref, ys_ref, acc_sc):
    i = pl.program_id(0)
    f = pl.program_id(1)

    @pl.when(f == 0)
    def _():
        acc_sc[...] = jnp.zeros(acc_sc.shape, F32)

    @pl.when(i < nu_ref[0])
    def _():
        h = xs_ref[...].astype(BF)
        act = (jax.nn.silu(_dot(h, wg_ref[...])) * _dot(h, wu_ref[...])).astype(BF)
        acc_sc[...] += _dot(act, wd_ref[...])

    @pl.when(f == pl.num_programs(1) - 1)
    def _():
        ys_ref[...] = acc_sc[...]


def _expert_ffn(xs, tile_expert, n_used, wg, wu, wd, tr, tf):
    n_rows, d = xs.shape
    dff = wg.shape[2]
    grid_spec = pltpu.PrefetchScalarGridSpec(
        num_scalar_prefetch=2,
        grid=(n_rows // tr, dff // tf),
        in_specs=[pl.BlockSpec((tr, d), lambda i, f, te, nu: (i, 0)),
                  pl.BlockSpec((None, d, tf), lambda i, f, te, nu: (te[i], 0, f)),
                  pl.BlockSpec((None, d, tf), lambda i, f, te, nu: (te[i], 0, f)),
                  pl.BlockSpec((None, tf, d), lambda i, f, te, nu: (te[i], f, 0))],
        out_specs=pl.BlockSpec((tr, d), lambda i, f, te, nu: (i, 0)),
        scratch_shapes=[pltpu.VMEM((tr, d), F32)],
    )
    return pl.pallas_call(
        _expert_ffn_kernel,
        grid_spec=grid_spec,
        out_shape=jax.ShapeDtypeStruct((n_rows, d), F32),
        compiler_params=_params("parallel", "arbitrary"),
        name="moe_experts",
    )(tile_expert, n_used, xs, wg, wu, wd)


def _combine_kernel(pos1_ref, pos2_ref, x_ref, route_ref, ys_ref, o_ref, buf, sems, *, tm):
    i = pl.program_id(0)
    n_steps = pl.num_programs(0)

    def copies_of(step, slot):
        def copies(t):
            return [pltpu.make_async_copy(ys_ref.at[pl.ds(pos_ref[step * tm + t], 1), :],
                                          buf.at[slot, k, pl.ds(t, 1), :], sems.at[slot])
                    for k, pos_ref in enumerate((pos1_ref, pos2_ref))]
        return copies

    @pl.when(i == 0)
    def _():
        _row_copy_loop(tm, copies_of(0, 0), wait=False)

    slot = i % 2
    _row_copy_loop(tm, copies_of(i, slot), wait=True)

    @pl.when(i + 1 < n_steps)
    def _():
        _row_copy_loop(tm, copies_of(i + 1, 1 - slot), wait=False)

    route = route_ref[...]
    o_ref[...] = x_ref[...] + route[:, 2:3] * buf[slot, 0] + route[:, 3:4] * buf[slot, 1]


def _combine(x, route, ys, pos1, pos2, tm):
    n, d = x.shape
    grid_spec = pltpu.PrefetchScalarGridSpec(
        num_scalar_prefetch=2,
        grid=(n // tm,),
        in_specs=[pl.BlockSpec((tm, d), lambda i, p1, p2: (i, 0)),
                  pl.BlockSpec((tm, LANES), lambda i, p1, p2: (i, 0)),
                  pl.BlockSpec(memory_space=pl.ANY)],
        out_specs=pl.BlockSpec((tm, d), lambda i, p1, p2: (i, 0)),
        scratch_shapes=[pltpu.VMEM((2, 2, tm, d), F32), pltpu.SemaphoreType.DMA((2,))],
    )
    return pl.pallas_call(
        functools.partial(_combine_kernel, tm=tm),
        grid_spec=grid_spec,
        out_shape=jax.ShapeDtypeStruct((n, d), F32),
        compiler_params=_params("arbitrary"),
        name="moe_combine",
    )(pos1, pos2, x, route, ys)


def _moe(hf, x, route, wg, wu, wd, tf):
    n, d = x.shape
    tr = MOE_ROW_TILE if 2 * n >= N_EXPERTS * MOE_ROW_TILE else 256
    pos1, pos2, tile_expert, n_used, n_tiles = _route_plan(route, tr)
    tm = _pick_tile(n, 256)
    xs = _dispatch(hf, pos1, pos2, n_tiles * tr, tm)
    ys = _expert_ffn(xs, tile_expert, n_used, wg, wu, wd, tr, tf)
    return _combine(x, route, ys, pos1, pos2, tm)


def _final_norm_kernel(x_ref, g_ref, o_ref):
    o_ref[...] = _rms(x_ref[...], g_ref[...])


def _final_norm(x, g, tm):
    n, d = x.shape
    return pl.pallas_call(
        _final_norm_kernel,
        grid=(n // tm,),
        in_specs=[pl.BlockSpec((tm, d), lambda i: (i, 0)), _const_spec(g.shape)],
        out_specs=pl.BlockSpec((tm, d), lambda i: (i, 0)),
        out_shape=jax.ShapeDtypeStruct((n, d), F32),
        compiler_params=_params("parallel"),
        name="final_norm",
    )(x, g)


def _prep_layer(w, l):
    d = w['w_in'].shape[1]
    lw = {}
    lw['g_mix'] = w['g_mix'][l][None, :]
    w_in = w['w_in'][l]
    split = SEG_KROPE[0] + ROPE_DIM
    lw['w_in_p'] = jnp.concatenate(
        [w_in[:, :split], jnp.zeros((d, ROPE_PAD), w_in.dtype), w_in[:, split:]], axis=1).astype(BF)
    lw['g_q'] = w['g_q'][l][None, :]
    lw['g_kv'] = w['g_kv'][l][None, :]
    half = ROPE_DIM // 2
    w3 = w['w_uq'][l].reshape(Q_RANK, MLA_HEADS, QK_NOPE + ROPE_DIM)
    lw['w_uq_p'] = jnp.concatenate(
        [w3[:, :, :QK_NOPE].reshape(Q_RANK, -1),
         w3[:, :, QK_NOPE:QK_NOPE + half].reshape(Q_RANK, -1),
         w3[:, :, QK_NOPE + half:].reshape(Q_RANK, -1)], axis=1).astype(BF)
    wk = jnp.transpose(w['w_uk'][l], (1, 2, 0))
    zk = jnp.zeros((QK_NOPE, KV_RANK), wk.dtype)
    lw['w_uk_bd'] = jnp.stack([
        jnp.concatenate([jnp.concatenate([wk[2 * p], zk], axis=1),
                         jnp.concatenate([zk, wk[2 * p + 1]], axis=1)], axis=0)
        for p in range(MLA_HEADS // 2)]).astype(BF)
    perm = np.zeros((2 * LANES, MLA_HEADS * LANES), np.float32)
    for h in range(MLA_HEADS):
        for i in range(half):
            perm[h * half + i, LANES * h + i] = 1.0
            perm[LANES + h * half + i, LANES * h + half + i] = 1.0
    lw['perm'] = jnp.asarray(perm, BF)
    eye_h = jnp.eye(MLA_HEADS, dtype=F32)
    lw['w_uv_bd'] = jnp.einsum('rhd,hg->hrgd', w['w_uv'][l], eye_h).reshape(
        MLA_HEADS * KV_RANK, MLA_HEADS * V_DIM).astype(BF)
    for name in ('w_br_a', 'w_br_b', 'w_br_c', 'w_out', 'ssm_w_glu'):
        lw[name] = w[name][l].astype(BF)
    lw['g_ffn'] = w['g_ffn'][l][None, :]

    lam_re, lam_im = w['ssm_lam_re'][l], w['ssm_lam_im'][l]
    dt = jnp.exp(w['ssm_log_dt'][l])[:, None]
    decay = jnp.exp(lam_re * dt)
    ab_re, ab_im = decay * jnp.cos(lam_im * dt), decay * jnp.sin(lam_im * dt)
    inv_den = 1.0 / (lam_re * lam_re + lam_im * lam_im)
    nr, ni = ab_re - 1.0, ab_im
    co_re = (nr * lam_re + ni * lam_im) * inv_den
    co_im = (ni * lam_re - nr * lam_im) * inv_den
    b_re, b_im = w['ssm_b_re'][l], w['ssm_b_im'][l]
    bb_re = co_re[..., None] * b_re - co_im[..., None] * b_im
    bb_im = co_re[..., None] * b_im + co_im[..., None] * b_re
    eye_g = jnp.eye(SSM_GROUPS, dtype=F32)
    bd = lambda bb: jnp.einsum('gnp,gh->gphn', bb, eye_g).reshape(SSM_WIDTH, SSM_FLAT)
    lw['ssm_bbd'] = jnp.concatenate([bd(bb_re), bd(bb_im)], axis=1).astype(BF)
    cd = lambda cc: jnp.einsum('gpn,gh->gnhp', cc, eye_g).reshape(SSM_FLAT, SSM_WIDTH)
    lw['ssm_cbd'] = jnp.concatenate([cd(w['ssm_c_re'][l]), -cd(w['ssm_c_im'][l])], axis=0).astype(BF)
    lw['ssm_a_re'] = ab_re.reshape(1, SSM_FLAT)
    lw['ssm_a_im'] = ab_im.reshape(1, SSM_FLAT)
    lw['ssm_d'] = w['ssm_d'][l][None, :]
    lw['ssm_b_glu'] = w['ssm_b_glu'][l][None, :]
    return lw


def _rope_tables(q_start, t, tm):
    half = ROPE_DIM // 2
    inv = 1.0 / (ROPE_THETA ** (jnp.arange(0, ROPE_DIM, 2, dtype=F32) / ROPE_DIM))
    ang = (q_start + jnp.arange(t)).astype(F32)[:, None] * inv[None, :]
    cos, sin = jnp.cos(ang), jnp.sin(ang)
    z16 = jnp.zeros_like(cos)
    zrest = jnp.zeros((t, LANES - ROPE_DIM), F32)
    tabs = [jnp.tile(cos, (1, MLA_HEADS)), jnp.tile(sin, (1, MLA_HEADS)),
            jnp.concatenate([cos, cos, zrest], axis=1),
            jnp.concatenate([-sin, z16, zrest], axis=1),
            jnp.concatenate([z16, sin, zrest], axis=1)]
    if tm > t:
        tabs = [jnp.tile(a, (tm // t, 1)) for a in tabs]
    return tabs


def _trunk(x, q_start, caches, w, layers, ffn_w):
    bsz, t, d = x.shape
    n = bsz * t
    xf = x.reshape(n, d)
    tm = _pick_tile(n, 256)
    assert t % tm == 0 or tm % t == 0
    tabs = _rope_tables(q_start, t, tm)
    new_rows = []
    for l, lw in enumerate(layers):
        (qcat, kcat, lat, krope, u, qsb0, qsb1, ksb, vsb, gates) = _in_proj(xf, lw, tabs, tm, bsz, t)
        time_major = u.shape[0] != n
        if time_major:
            u_tm = u.reshape(n, SSM_WIDTH)
        else:
            u_tm = u.reshape(bsz, t, SSM_WIDTH).transpose(1, 0, 2).reshape(n, SSM_WIDTH)
        if caches is None:
            olat = _mla_prompt(qcat, kcat, bsz, t)
            oc0, oc1 = _sb_prompt(qsb0, qsb1, ksb, vsb, bsz, t)
            h0_re = jnp.zeros((bsz, SSM_FLAT), F32)
            h0_im = h0_re
        else:
            lat_pool, kr_pool, k_pool, v_pool, s_re, s_im, page_table = caches
            olat = _mla_sample(qcat, kcat, lat_pool, kr_pool, page_table, l, bsz, t)
            oc0, oc1 = _sb_sample(qsb0, qsb1, ksb, vsb, k_pool, v_pool, page_table, l, bsz, t)
            h0_re = s_re[l].reshape(bsz, SSM_FLAT)
            h0_im = s_im[l].reshape(bsz, SSM_FLAT)
        ob_tm, h_re, h_im = _s5(u_tm, h0_re, h0_im, lw, bsz, t)
        if time_major:
            ob = ob_tm.reshape(t, bsz * SSM_WIDTH)
        else:
            ob = ob_tm.reshape(t, bsz, SSM_WIDTH).transpose(1, 0, 2).reshape(n, SSM_WIDTH)
        i = l // 2
        tm_f = _pick_tile(n, 1024)
        if l % 2 == 0:
            x1, hf = _merge(xf, olat, ob, oc0, oc1, gates, lw, None, tm, t)
            wg, wu, wd = ffn_w['dense'][i]
            xf = _ffn(hf, x1, wg, wu, wd, tm_f, 256)
        else:
            x1, hf, gate = _merge(xf, olat, ob, oc0, oc1, gates, lw, ffn_w['router'][i], tm, t)
            wg, wu, wd = ffn_w['moe'][i]
            xf = _moe(hf, x1, gate, wg, wu, wd, 512)
        new_rows.append((lat.reshape(bsz, t, KV_RANK), krope.reshape(bsz, t, ROPE_DIM),
                         ksb.reshape(bsz, t, SB_KV_HEADS, SB_HEAD_DIM),
                         vsb.reshape(bsz, t, SB_KV_HEADS, SB_HEAD_DIM),
                         h_re.reshape(bsz, SSM_GROUPS, SSM_STATE),
                         h_im.reshape(bsz, SSM_GROUPS, SSM_STATE)))
    y = _final_norm(xf, w['g_final'][None, :], tm).reshape(bsz, t, d)
    stacked = [jnp.stack([r[j] for r in new_rows]) for j in range(6)]
    return y, stacked


def kernel(x_prompt, x_sample, cache_mla_latent, cache_mla_krope, cache_sb_k, cache_sb_v, state_ssm_re, state_ssm_im, page_table, g_mix, w_in, g_q, g_kv, w_uq, w_uk, w_uv, ssm_lam_re, ssm_lam_im, ssm_log_dt, ssm_b_re, ssm_b_im, ssm_c_re, ssm_c_im, ssm_d, ssm_w_glu, ssm_b_glu, w_br_a, w_br_b, w_br_c, w_out, g_ffn, ffn_w_gate, ffn_w_up, ffn_w_down, moe_router, moe_w_gate, moe_w_up, moe_w_down, g_final):
    w = dict(g_mix=g_mix, w_in=w_in, g_q=g_q, g_kv=g_kv, w_uq=w_uq, w_uk=w_uk, w_uv=w_uv,
             ssm_lam_re=ssm_lam_re, ssm_lam_im=ssm_lam_im, ssm_log_dt=ssm_log_dt,
             ssm_b_re=ssm_b_re, ssm_b_im=ssm_b_im, ssm_c_re=ssm_c_re, ssm_c_im=ssm_c_im,
             ssm_d=ssm_d, ssm_w_glu=ssm_w_glu, ssm_b_glu=ssm_b_glu, w_br_a=w_br_a,
             w_br_b=w_br_b, w_br_c=w_br_c, w_out=w_out, g_ffn=g_ffn, g_final=g_final)
    depth = w_in.shape[0]
    layers = [_prep_layer(w, l) for l in range(depth)]
    d = w_in.shape[1]
    router = jnp.pad(moe_router, ((0, 0), (0, 0), (0, LANES - N_EXPERTS)))
    ffn_w = {
        'dense': [(ffn_w_gate[i].astype(BF), ffn_w_up[i].astype(BF), ffn_w_down[i].astype(BF))
                  for i in range(ffn_w_gate.shape[0])],
        'moe': [(moe_w_gate[i].astype(BF), moe_w_up[i].astype(BF), moe_w_down[i].astype(BF))
                for i in range(moe_w_gate.shape[0])],
        'router': [router[i] for i in range(router.shape[0])],
    }
    past_len = page_table.shape[1] * PAGE_KEYS
    y_p, (lat_p, kr_p, k_p, v_p, re_p, im_p) = _trunk(x_prompt, 0, None, w, layers, ffn_w)
    caches = (cache_mla_latent, cache_mla_krope, cache_sb_k, cache_sb_v,
              state_ssm_re, state_ssm_im, page_table)
    y_s, (lat_s, kr_s, k_s, v_s, re_s, im_s) = _trunk(x_sample, past_len, caches, w, layers, ffn_w)
    return (y_p, y_s, lat_p, lat_s, kr_p, kr_s, k_p, k_s, v_p, v_s, re_p, re_s, im_p, im_s)
```

```python
import functools
import math

import numpy as np
import jax
import jax.numpy as jnp
from jax import lax
from jax.experimental import pallas as pl
from jax.experimental.pallas import tpu as pltpu

F32 = jnp.float32
BF = jnp.bfloat16

MLA_HEADS = 8
QK_NOPE = 64
ROPE_DIM = 32
V_DIM = 64
Q_RANK = 256
KV_RANK = 128
ROPE_THETA = 10000.0
MLA_SCALE = (QK_NOPE + ROPE_DIM) ** -0.5
SSM_WIDTH = 512
SSM_GROUP_SIZE = 16
SSM_GROUPS = SSM_WIDTH // SSM_GROUP_SIZE
SSM_STATE = 64
SSM_FLAT = SSM_GROUPS * SSM_STATE
SB_HEADS = 8
SB_KV_HEADS = 2
SB_HEAD_DIM = 64
SB_GROUP = SB_HEADS // SB_KV_HEADS
SB_SCALE = SB_HEAD_DIM ** -0.5
N_EXPERTS = 8
EPS = 1e-6
NEG_BIG = -1e30
LOG2E = math.log2(math.e)
MLA_Q_SCALE = MLA_SCALE * LOG2E
SB_Q_SCALE = SB_SCALE * LOG2E

LANES = 128
SUBLANES = 8
VMEM_LIMIT_BYTES = 56 * 1024 * 1024

ROPE_PAD = LANES - ROPE_DIM
SEG_CQ = (0, 256)
SEG_CKV = (256, 384)
SEG_KROPE = (384, 512)
SEG_U = (512, 1024)
SEG_QSB = (1024, 1536)
SEG_KSB = (1536, 1664)
SEG_VSB = (1664, 1792)
SEG_GATES = (1792, 4864)
IN_PADDED = 4864

PAGE_KEYS = 128
SAMPLE_PAGES_PER_STEP = 64
SB_KEY_BLOCK = 256
MLA_KEY_BLOCK = 512
MLA_Q_POS_BLOCK = 256
MOE_ROW_TILE = 1024
ROW_COPY_UNROLL = 8
MLA_ROW_SUB = 128
MLA_SCORE_LOOKAHEAD = 4
Q_POS_BLOCK = 256


def _params(*sem):
    return pltpu.CompilerParams(dimension_semantics=sem, vmem_limit_bytes=VMEM_LIMIT_BYTES)


def _pick_tile(n, pref):
    t = min(n, pref)
    while n % t:
        t -= SUBLANES
    return t


def _rms(x, g):
    return x * lax.rsqrt(jnp.mean(x * x, axis=-1, keepdims=True) + EPS) * g


def _dot(a, b):
    return jnp.dot(a, b, preferred_element_type=F32)


def _dot_nt(a, b):
    return lax.dot_general(a, b, (((1,), (1,)), ((), ())), preferred_element_type=F32)


def _const_spec(shape):
    nd = len(shape)
    return pl.BlockSpec(shape, lambda *_: (0,) * nd)


def _in_proj_kernel(x_ref, gmix_ref, win_ref, gq_ref, gkv_ref, wuq_ref, wukbd_ref, perm_ref,
                    cos8_ref, sin8_ref, ka_ref, kb_ref, kc_ref,
                    qcat_ref, kcat_ref, lat_ref, kr_ref, u_ref, qsb0_ref, qsb1_ref,
                    ksb_ref, vsb_ref, gates_ref):
    h = _rms(x_ref[...], gmix_ref[...]).astype(BF)

    def seg(bounds):
        return _dot(h, win_ref[:, bounds[0]:bounds[1]])

    qn = _rms(seg(SEG_CQ), gq_ref[...]).astype(BF)
    q = _dot(qn, wuq_ref[...])
    n_nope = MLA_HEADS * QK_NOPE
    x1 = q[:, n_nope:n_nope + LANES]
    x2 = q[:, n_nope + LANES:n_nope + 2 * LANES]
    c8 = cos8_ref[...]
    s8 = sin8_ref[...]
    rot = (jnp.concatenate([x1 * c8 - x2 * s8, x1 * s8 + x2 * c8], axis=1) * MLA_Q_SCALE).astype(BF)
    qr = _dot(rot, perm_ref[...])
    for pair in range(MLA_HEADS // 2):
        qn_pair = q[:, LANES * pair:LANES * (pair + 1)].astype(BF)
        ql = _dot(qn_pair, wukbd_ref[pair]) * MLA_Q_SCALE
        for j in range(2):
            hd = 2 * pair + j
            qcat_ref[hd, :, :LANES] = ql[:, LANES * j:LANES * (j + 1)].astype(BF)
            qcat_ref[hd, :, LANES:] = qr[:, LANES * hd:LANES * (hd + 1)].astype(BF)

    lat = _rms(seg(SEG_CKV), gkv_ref[...])
    lat_ref[...] = lat
    zk = seg(SEG_KROPE)
    half = ROPE_DIM // 2
    kr = (zk * ka_ref[...] + pltpu.roll(zk, LANES - half, 1) * kb_ref[...]
          + pltpu.roll(zk, half, 1) * kc_ref[...])
    kr_ref[...] = kr[:, :ROPE_DIM]
    kcat_ref[:, :LANES] = lat.astype(BF)
    kcat_ref[:, LANES:] = kr.astype(BF)

    u_ref[...] = seg(SEG_U)
    qs = seg(SEG_QSB) * SB_Q_SCALE
    hw = SB_GROUP * SB_HEAD_DIM
    qsb0_ref[...] = qs[:, :hw].astype(BF)
    qsb1_ref[...] = qs[:, hw:].astype(BF)
    ksb_ref[...] = seg(SEG_KSB)
    vsb_ref[...] = seg(SEG_VSB)
    g0 = SEG_GATES[0]
    d = (SEG_GATES[1] - g0) // 3
    for j in range(3):
        gates_ref[:, d * j:d * (j + 1)] = jax.nn.sigmoid(seg((g0 + d * j, g0 + d * (j + 1)))).astype(BF)


def _time_major_spec(tm, t):
    nt = t // tm
    return pl.BlockSpec((tm, SSM_WIDTH), lambda i: (i % nt, i // nt))


def _in_proj(x, lw, tabs, tm, bsz, t):
    n, d = x.shape
    nblk = tabs[0].shape[0] // tm
    row = lambda w: pl.BlockSpec((tm, w), lambda i: (i, 0))
    tab = pl.BlockSpec((tm, LANES), lambda i: (i % nblk, 0))
    hw = SB_GROUP * SB_HEAD_DIM
    kvw = SB_KV_HEADS * SB_HEAD_DIM
    u_time_major = t % tm == 0
    out_shape = (
        jax.ShapeDtypeStruct((MLA_HEADS, n, 256), BF),
        jax.ShapeDtypeStruct((n, 256), BF),
        jax.ShapeDtypeStruct((n, KV_RANK), F32),
        jax.ShapeDtypeStruct((n, ROPE_DIM), F32),
        jax.ShapeDtypeStruct((t, bsz * SSM_WIDTH) if u_time_major else (n, SSM_WIDTH), F32),
        jax.ShapeDtypeStruct((n, hw), BF),
        jax.ShapeDtypeStruct((n, hw), BF),
        jax.ShapeDtypeStruct((n, kvw), F32),
        jax.ShapeDtypeStruct((n, kvw), F32),
        jax.ShapeDtypeStruct((n, 3 * d), BF),
    )
    out_specs = [row(s.shape[-1]) for s in out_shape]
    out_specs[0] = pl.BlockSpec((MLA_HEADS, tm, 256), lambda i: (0, i, 0))
    if u_time_major:
        out_specs[4] = _time_major_spec(tm, t)
    out_specs = tuple(out_specs)
    consts = [lw['g_mix'], lw['w_in_p'], lw['g_q'], lw['g_kv'], lw['w_uq_p'], lw['w_uk_bd'], lw['perm']]
    return pl.pallas_call(
        _in_proj_kernel,
        grid=(n // tm,),
        in_specs=[row(d)] + [_const_spec(c.shape) for c in consts] + [tab] * 5,
        out_specs=out_specs,
        out_shape=out_shape,
        compiler_params=_params("parallel"),
        name="in_proj",
    )(x, *consts, *tabs)


def _mla_prompt_kernel(qi_ref, kj_ref, last_ref, q_ref, k_ref, o_ref, m_sc, acc_sc, *, tq, kb, rs):
    p = pl.program_id(1)
    qi = qi_ref[p]
    kj = kj_ref[p]
    @pl.when(kj == 0)
    def _():
        m_sc[...] = jnp.full(m_sc.shape, NEG_BIG, F32)
        acc_sc[...] = jnp.zeros(acc_sc.shape, F32)

    k = k_ref[...]
    v_ext = jnp.concatenate([k[:, :KV_RANK], jnp.ones((kb, LANES), BF)], axis=1)

    def process(masked):
        per_head = tq // rs
        subs = [(h, c * rs) for h in range(MLA_HEADS) for c in range(per_head)]
        pending = []
        for step in range(len(subs) + MLA_SCORE_LOOKAHEAD):
            if step < len(subs):
                h, lo = subs[step]
                pending.append(_dot_nt(q_ref[h, lo:lo + rs, :], k))
            r = step - MLA_SCORE_LOOKAHEAD
            if r < 0:
                continue
            h, lo = subs[r]
            s = pending.pop(0)
            if masked:
                row = lax.broadcasted_iota(jnp.int32, s.shape, 0)
                col = lax.broadcasted_iota(jnp.int32, s.shape, 1)
                s = jnp.where(kj * kb + col <= qi * tq + lo + row, s, NEG_BIG)
            m_prev = m_sc[h, lo:lo + rs, :]
            m_new = jnp.maximum(m_prev, jnp.max(s, axis=1, keepdims=True))
            alpha = jnp.exp2(m_prev - m_new)
            pr = jnp.exp2(s - m_new).astype(BF)
            acc_sc[h, lo:lo + rs, :] = alpha * acc_sc[h, lo:lo + rs, :] + _dot(pr, v_ext)
            m_sc[h, lo:lo + rs, :] = m_new

    needs_mask = kj * kb + kb - 1 > qi * tq

    @pl.when(needs_mask)
    def _():
        process(True)

    @pl.when(jnp.logical_not(needs_mask))
    def _():
        process(False)

    @pl.when(last_ref[p] == 1)
    def _():
        for h in range(MLA_HEADS):
            acc = acc_sc[h]
            o_ref[h] = (acc[:, :KV_RANK] / acc[:, KV_RANK:KV_RANK + 1]).astype(BF)


def _causal_pairs(t, tq, kb, strict, descending):
    qi_l, kj_l, first_l, last_l = [], [], [], []
    for qi in range(t // tq):
        last_key = qi * tq + tq - 1 - (1 if strict else 0)
        ks = list(range(last_key // kb + 1))
        if descending:
            ks = ks[::-1]
        for n, kj in enumerate(ks):
            qi_l.append(qi)
            kj_l.append(kj)
            first_l.append(1 if n == 0 else 0)
            last_l.append(1 if n == len(ks) - 1 else 0)
    as_arr = lambda v: jnp.asarray(np.asarray(v, np.int32))
    return as_arr(qi_l), as_arr(kj_l), as_arr(first_l), as_arr(last_l)


def _mla_prompt(qcat, kcat, bsz, t):
    tq = min(MLA_Q_POS_BLOCK, t)
    kb = min(MLA_KEY_BLOCK, t)
    qi_t, kj_t, _, last_t = _causal_pairs(t, tq, kb, strict=False, descending=False)
    rs = min(MLA_ROW_SUB, tq)
    nq, nk = t // tq, t // kb
    grid_spec = pltpu.PrefetchScalarGridSpec(
        num_scalar_prefetch=3,
        grid=(bsz, int(qi_t.shape[0])),
        in_specs=[
            pl.BlockSpec((MLA_HEADS, tq, 256), lambda b, p, qi, kj, la: (0, b * nq + qi[p], 0)),
            pl.BlockSpec((kb, 256), lambda b, p, qi, kj, la: (b * nk + kj[p], 0)),
        ],
        out_specs=pl.BlockSpec((MLA_HEADS, tq, KV_RANK), lambda b, p, qi, kj, la: (0, b * nq + qi[p], 0)),
        scratch_shapes=[pltpu.VMEM((MLA_HEADS, tq, 1), F32), pltpu.VMEM((MLA_HEADS, tq, 2 * LANES), F32)],
    )
    return pl.pallas_call(
        functools.partial(_mla_prompt_kernel, tq=tq, kb=kb, rs=rs),
        grid_spec=grid_spec,
        out_shape=jax.ShapeDtypeStruct((MLA_HEADS, bsz * t, KV_RANK), BF),
        compiler_params=_params("parallel", "arbitrary"),
        name="mla_prompt",
    )(qi_t, kj_t, last_t, qcat, kcat)


def _stream_page_chunks(pt_ref, pools, bufs, sems, chunk_of_step, layer, pb, page_axis, compute):
    b = pl.program_id(0)
    j = pl.program_id(1)
    nb = pl.num_programs(0)
    nch = pl.num_programs(1) - 1

    def copies(bb, jj, slot):
        chunk = chunk_of_step(jj)
        out = []
        for i in range(pb):
            page = pt_ref[bb, chunk * pb + i]
            for a, (pool, buf) in enumerate(zip(pools, bufs)):
                extent = pool.shape[2 + page_axis[a]]
                window = pl.ds(i * extent, extent)
                dst = buf.at[slot, window, :] if page_axis[a] == 0 else buf.at[slot, :, window]
                out.append(pltpu.make_async_copy(pool.at[layer, page], dst, sems.at[a, slot]))
        return out

    @pl.when((j == 0) & (b == 0))
    def _():
        for cp in copies(0, 1, 0):
            cp.start()

    @pl.when(j > 0)
    def _():
        slot = (b * nch + j - 1) % 2
        for cp in copies(b, j, slot):
            cp.wait()

        @pl.when(j < nch)
        def _():
            for cp in copies(b, j + 1, 1 - slot):
                cp.start()

        @pl.when((j == nch) & (b + 1 < nb))
        def _():
            for cp in copies(b + 1, 1, 1 - slot):
                cp.start()

        compute(slot)


def _mla_sample_kernel(pt_ref, q_ref, newk_ref, lat_hbm, kr_hbm, o_ref, lat_buf, kr_buf, sems,
                       m_sc, l_sc, acc_sc, *, pb, layer):
    j = pl.program_id(1)
    q = q_ref[...]

    def update(s, vals):
        m_prev = m_sc[...]
        m_new = jnp.maximum(m_prev, jnp.max(s, axis=1, keepdims=True))
        alpha = jnp.exp2(m_prev - m_new)
        pr = jnp.exp2(s - m_new)
        l_sc[...] = alpha * l_sc[...] + jnp.sum(pr, axis=1, keepdims=True)
        acc_sc[...] = alpha * acc_sc[...] + _dot(pr.astype(BF), vals)
        m_sc[...] = m_new

    @pl.when(j == 0)
    def _():
        m_sc[...] = jnp.full(m_sc.shape, NEG_BIG, F32)
        l_sc[...] = jnp.zeros(l_sc.shape, F32)
        acc_sc[...] = jnp.zeros(acc_sc.shape, F32)
        nk = newk_ref[...]
        s = _dot_nt(q, nk)
        row = lax.broadcasted_iota(jnp.int32, s.shape, 0)
        col = lax.broadcasted_iota(jnp.int32, s.shape, 1)
        s = jnp.where(col <= row // MLA_HEADS, s, NEG_BIG)
        update(s, nk[:, :KV_RANK])

    def paged(slot):
        q_lat = q[:, :KV_RANK]
        q_rope = q[:, KV_RANK:KV_RANK + ROPE_DIM]
        lat = lat_buf[slot].astype(BF)
        kr_t = kr_buf[slot].astype(BF)
        update(_dot_nt(q_lat, lat) + _dot(q_rope, kr_t), lat)

    _stream_page_chunks(pt_ref, (lat_hbm, kr_hbm), (lat_buf, kr_buf), sems, lambda jj: jj - 1,
                        layer, pb, (0, 1), paged)

    @pl.when(j == pl.num_programs(1) - 1)
    def _():
        o_ref[...] = (acc_sc[...] / l_sc[...]).astype(BF)


def _mla_sample(qcat, kcat, lat_pool, kr_pool, page_table, layer, bsz, t):
    n_pages = page_table.shape[1]
    pb = min(SAMPLE_PAGES_PER_STEP, n_pages)
    nch = n_pages // pb
    rows = t * MLA_HEADS
    q3 = jnp.transpose(qcat, (1, 0, 2)).reshape(bsz, rows, 256)
    newk = jnp.pad(kcat.reshape(bsz, t, 256), ((0, 0), (0, PAGE_KEYS - t), (0, 0)))
    kr_pool = jnp.swapaxes(kr_pool, 2, 3)
    hbm = pl.BlockSpec(memory_space=pl.ANY)
    grid_spec = pltpu.PrefetchScalarGridSpec(
        num_scalar_prefetch=1,
        grid=(bsz, nch + 1),
        in_specs=[pl.BlockSpec((None, rows, 256), lambda b, j, pt: (b, 0, 0)),
                  pl.BlockSpec((None, PAGE_KEYS, 256), lambda b, j, pt: (b, 0, 0)),
                  hbm, hbm],
        out_specs=pl.BlockSpec((None, rows, KV_RANK), lambda b, j, pt: (b, 0, 0)),
        scratch_shapes=[pltpu.VMEM((2, pb * PAGE_KEYS, KV_RANK), F32),
                        pltpu.VMEM((2, ROPE_DIM, pb * PAGE_KEYS), F32),
                        pltpu.SemaphoreType.DMA((2, 2)),
                        pltpu.VMEM((rows, 1), F32), pltpu.VMEM((rows, 1), F32),
                        pltpu.VMEM((rows, KV_RANK), F32)],
    )
    o = pl.pallas_call(
        functools.partial(_mla_sample_kernel, pb=pb, layer=layer),
        grid_spec=grid_spec,
        out_shape=jax.ShapeDtypeStruct((bsz, rows, KV_RANK), BF),
        compiler_params=_params("arbitrary", "arbitrary"),
        name="mla_sample",
    )(page_table, q3, newk, lat_pool, kr_pool)
    return jnp.transpose(o.reshape(bsz * t, MLA_HEADS, KV_RANK), (1, 0, 2))


def _sb_log2_1m(z, mask):
    log_1m = -(jnp.maximum(z, 0.0) + jnp.log2(1.0 + jnp.exp2(-jnp.abs(z))))
    if mask is not None:
        log_1m = jnp.where(mask, log_1m, 0.0)
    return log_1m


def _suffix_sums(log_1m, tri, two_terms=True):
    hi = log_1m.astype(BF)
    if not two_terms:
        return _dot(hi, tri)
    lo = (log_1m - hi.astype(F32)).astype(BF)
    return _dot(hi, tri) + _dot(lo, tri)


def _sb_weights(z, log_1m, between, mask):
    a = jnp.exp2(z + log_1m + between)
    if mask is not None:
        a = jnp.where(mask, a, 0.0)
    return a.astype(BF)


def _sb_prompt_kernel(qi_ref, kj_ref, first_ref, last_ref, q0_ref, q1_ref, k_ref, v_ref, tri_ref,
                      o0_ref, o1_ref, carry_sc, acc_sc, *, tq, kb):
    p = pl.program_id(1)
    qi = qi_ref[p]
    kj = kj_ref[p]

    @pl.when(first_ref[p] == 1)
    def _():
        carry_sc[...] = jnp.zeros(carry_sc.shape, F32)
        acc_sc[...] = jnp.zeros(acc_sc.shape, F32)

    rows = tq * SB_GROUP

    def process(masked):
        mask = None
        if masked:
            row = lax.broadcasted_iota(jnp.int32, (rows, kb), 0)
            col = lax.broadcasted_iota(jnp.int32, (rows, kb), 1)
            mask = (kj * kb + col) < (qi * tq + row // SB_GROUP)
        tri = tri_ref[...]
        kf = k_ref[...]
        vf = v_ref[...]
        for kv, q_ref in enumerate((q0_ref, q1_ref)):
            k = kf[:, SB_HEAD_DIM * kv:SB_HEAD_DIM * (kv + 1)].astype(BF)
            v = vf[:, SB_HEAD_DIM * kv:SB_HEAD_DIM * (kv + 1)].astype(BF)
            z = _dot_nt(q_ref[...], k)
            log_1m = _sb_log2_1m(z, mask)
            w = _suffix_sums(log_1m, tri, two_terms=False)
            carry = carry_sc[kv]
            a = _sb_weights(z, log_1m, carry + w, mask)
            acc_sc[kv] = acc_sc[kv] + _dot(a, v)
            carry_sc[kv] = carry + (w + log_1m)[:, 0:1]

    needs_mask = kj * kb + kb - 1 >= qi * tq

    @pl.when(needs_mask)
    def _():
        process(True)

    @pl.when(jnp.logical_not(needs_mask))
    def _():
        process(False)

    @pl.when(last_ref[p] == 1)
    def _():
        o0_ref[...] = acc_sc[0].astype(BF)
        o1_ref[...] = acc_sc[1].astype(BF)


def _strict_lower_ones(n):
    j = np.arange(n)[:, None]
    s = np.arange(n)[None, :]
    return jnp.asarray((j > s).astype(np.float32), BF)


def _sb_prompt(qsb0, qsb1, ksb, vsb, bsz, t):
    tq = min(Q_POS_BLOCK, t)
    kb = min(SB_KEY_BLOCK, t)
    qi_t, kj_t, first_t, last_t = _causal_pairs(t, tq, kb, strict=True, descending=True)
    rows = tq * SB_GROUP
    nq, nk = t // tq, t // kb
    n = bsz * t
    q0 = qsb0.reshape(n * SB_GROUP, SB_HEAD_DIM)
    q1 = qsb1.reshape(n * SB_GROUP, SB_HEAD_DIM)
    kvw = SB_KV_HEADS * SB_HEAD_DIM
    qspec = pl.BlockSpec((rows, SB_HEAD_DIM), lambda b, p, qi, kj, fi, la: (b * nq + qi[p], 0))
    kspec = pl.BlockSpec((kb, kvw), lambda b, p, qi, kj, fi, la: (b * nk + kj[p], 0))
    grid_spec = pltpu.PrefetchScalarGridSpec(
        num_scalar_prefetch=4,
        grid=(bsz, int(qi_t.shape[0])),
        in_specs=[qspec, qspec, kspec, kspec,
                  pl.BlockSpec((kb, kb), lambda b, p, qi, kj, fi, la: (0, 0))],
        out_specs=(qspec, qspec),
        scratch_shapes=[pltpu.VMEM((SB_KV_HEADS, rows, 1), F32),
                        pltpu.VMEM((SB_KV_HEADS, rows, SB_HEAD_DIM), F32)],
    )
    o_sds = jax.ShapeDtypeStruct((n * SB_GROUP, SB_HEAD_DIM), BF)
    o0, o1 = pl.pallas_call(
        functools.partial(_sb_prompt_kernel, tq=tq, kb=kb),
        grid_spec=grid_spec,
        out_shape=(o_sds, o_sds),
        compiler_params=_params("parallel", "arbitrary"),
        name="sb_prompt",
    )(qi_t, kj_t, first_t, last_t, q0, q1, ksb, vsb, _strict_lower_ones(kb))
    hw = SB_GROUP * SB_HEAD_DIM
    return o0.reshape(n, hw), o1.reshape(n, hw)


def _sb_sample_kernel(pt_ref, q_ref, newk_ref, newv_ref, tri_new_ref, tri_ref, k_hbm, v_hbm,
                      o_ref, k_buf, v_buf, sems, carry_sc, acc_sc, *, pb, t_new, layer):
    j = pl.program_id(1)
    q = q_ref[...]
    rows = q.shape[0]
    rows_per_kv = t_new * SB_GROUP

    @pl.when(j == 0)
    def _():
        nk_t = newk_ref[...].astype(BF)
        nv_t = newv_ref[...].astype(BF)
        row = lax.broadcasted_iota(jnp.int32, (rows, PAGE_KEYS), 0)
        col = lax.broadcasted_iota(jnp.int32, (rows, PAGE_KEYS), 1)
        mask = col < (row % rows_per_kv) // SB_GROUP
        z = _dot(q, nk_t)
        log_1m = _sb_log2_1m(z, mask)
        w = _suffix_sums(log_1m, tri_new_ref[...])
        a = _sb_weights(z, log_1m, w, mask)
        acc_sc[...] = _dot_nt(a, nv_t)
        carry_sc[...] = (w + log_1m)[:, 0:1]

    def paged(slot):
        k_t = k_buf[slot].astype(BF)
        v_t = v_buf[slot].astype(BF)
        z = _dot(q, k_t)
        log_1m = _sb_log2_1m(z, None)
        sub = tri_ref.shape[0]
        nsub = z.shape[1] // sub
        stacked = jnp.concatenate([log_1m[:, sub * s:sub * (s + 1)] for s in range(nsub)], axis=0)
        w = _suffix_sums(stacked, tri_ref[...])
        totals = (w + stacked)[:, 0:1]
        carry = carry_sc[...]
        between = [None] * nsub
        for s in reversed(range(nsub)):
            between[s] = carry + w[rows * s:rows * (s + 1), :]
            carry = carry + totals[rows * s:rows * (s + 1), :]
        a = _sb_weights(z, log_1m, jnp.concatenate(between, axis=1), None)
        acc_sc[...] = acc_sc[...] + _dot_nt(a, v_t)
        carry_sc[...] = carry

    _stream_page_chunks(pt_ref, (k_hbm, v_hbm), (k_buf, v_buf), sems,
                        lambda jj: pl.num_programs(1) - 1 - jj, layer, pb, (1, 1), paged)

    @pl.when(j == pl.num_programs(1) - 1)
    def _():
        o_ref[...] = acc_sc[...]


def _sb_sample(qsb0, qsb1, ksb, vsb, k_pool, v_pool, page_table, layer, bsz, t):
    n_pages = page_table.shape[1]
    pb = min(SAMPLE_PAGES_PER_STEP, n_pages)
    nch = n_pages // pb
    kvw = SB_KV_HEADS * SB_HEAD_DIM
    rows = SB_KV_HEADS * t * SB_GROUP
    sub = min(SB_KEY_BLOCK, pb * PAGE_KEYS)
    q0 = qsb0.reshape(bsz, t * SB_GROUP, SB_HEAD_DIM)
    q1 = qsb1.reshape(bsz, t * SB_GROUP, SB_HEAD_DIM)
    zq = jnp.zeros_like(q0)
    q = jnp.concatenate([jnp.concatenate([q0, zq], axis=2), jnp.concatenate([zq, q1], axis=2)], axis=1)
    pad = ((0, 0), (0, 0), (0, PAGE_KEYS - t))
    newk = jnp.pad(jnp.swapaxes(ksb.reshape(bsz, t, kvw), 1, 2), pad)
    newv = jnp.pad(jnp.swapaxes(vsb.reshape(bsz, t, kvw), 1, 2), pad)
    depth, n_phys = k_pool.shape[0], k_pool.shape[1]
    kp = jnp.transpose(k_pool, (0, 1, 3, 4, 2)).reshape(depth, n_phys, kvw, PAGE_KEYS)
    vp = jnp.transpose(v_pool, (0, 1, 3, 4, 2)).reshape(depth, n_phys, kvw, PAGE_KEYS)

    per_b = lambda r, w: pl.BlockSpec((None, r, w), lambda b, j, pt: (b, 0, 0))
    const2 = lambda r: pl.BlockSpec((r, r), lambda b, j, pt: (0, 0))
    hbm = pl.BlockSpec(memory_space=pl.ANY)
    grid_spec = pltpu.PrefetchScalarGridSpec(
        num_scalar_prefetch=1,
        grid=(bsz, nch + 1),
        in_specs=[per_b(rows, kvw), per_b(kvw, PAGE_KEYS), per_b(kvw, PAGE_KEYS),
                  const2(PAGE_KEYS), const2(sub), hbm, hbm],
        out_specs=per_b(rows, kvw),
        scratch_shapes=[pltpu.VMEM((2, kvw, pb * PAGE_KEYS), F32),
                        pltpu.VMEM((2, kvw, pb * PAGE_KEYS), F32),
                        pltpu.SemaphoreType.DMA((2, 2)),
                        pltpu.VMEM((rows, 1), F32), pltpu.VMEM((rows, kvw), F32)],
    )
    o = pl.pallas_call(
        functools.partial(_sb_sample_kernel, pb=pb, t_new=t, layer=layer),
        grid_spec=grid_spec,
        out_shape=jax.ShapeDtypeStruct((bsz, rows, kvw), F32),
        compiler_params=_params("arbitrary", "arbitrary"),
        name="sb_sample",
    )(page_table, q, newk, newv, _strict_lower_ones(PAGE_KEYS), _strict_lower_ones(sub), kp, vp)
    o = o.reshape(bsz, SB_KV_HEADS, t * SB_GROUP, SB_KV_HEADS, SB_HEAD_DIM)
    hw = SB_GROUP * SB_HEAD_DIM
    o0 = o[:, 0, :, 0, :].reshape(bsz * t, hw).astype(BF)
    o1 = o[:, 1, :, 1, :].reshape(bsz * t, hw).astype(BF)
    return o0, o1


def _s5_kernel(u_ref, h0re_ref, h0im_ref, are_ref, aim_ref, bbd_ref, cbd_ref, dskip_ref, wglu_ref,
               bglu_ref, ob_ref, hre_ref, him_ref, s_sc, st_sc, *, bn, tc, lc):
    c = pl.program_id(0)

    @pl.when(c == 0)
    def _():
        st_sc[0] = h0re_ref[...]
        st_sc[1] = h0im_ref[...]

    u = u_ref[...]
    ub = u.astype(BF)
    nflat = SSM_FLAT
    for jb in range(2 * nflat // 512):
        s_sc[:, 512 * jb:512 * (jb + 1)] = _dot(ub, bbd_ref[:, 512 * jb:512 * (jb + 1)])

    for ci in range(nflat // lc):
        lo = ci * lc
        ar = jnp.broadcast_to(are_ref[:, lo:lo + lc], (bn, lc))
        ai = jnp.broadcast_to(aim_ref[:, lo:lo + lc], (bn, lc))

        def body(t, carry, lo=lo, ar=ar, ai=ai):
            hr, hi = carry
            r0 = pl.multiple_of(t * bn, bn)
            br = s_sc[pl.ds(r0, bn), lo:lo + lc]
            bi = s_sc[pl.ds(r0, bn), nflat + lo:nflat + lo + lc]
            nr = ar * hr - ai * hi + br
            ni = ar * hi + ai * hr + bi
            s_sc[pl.ds(r0, bn), lo:lo + lc] = nr
            s_sc[pl.ds(r0, bn), nflat + lo:nflat + lo + lc] = ni
            return nr, ni

        hr, hi = lax.fori_loop(0, tc, body, (st_sc[0, :, lo:lo + lc], st_sc[1, :, lo:lo + lc]))
        st_sc[0, :, lo:lo + lc] = hr
        st_sc[1, :, lo:lo + lc] = hi

    y = dskip_ref[...] * u
    for jb in range(2 * nflat // 1024):
        y = y + _dot(s_sc[:, 1024 * jb:1024 * (jb + 1)].astype(BF), cbd_ref[1024 * jb:1024 * (jb + 1), :])
    g = jax.nn.gelu(y)
    gate = jax.nn.sigmoid(_dot(g.astype(BF), wglu_ref[...]) + bglu_ref[...])
    ob_ref[...] = (g * gate).astype(BF)

    @pl.when(c == pl.num_programs(0) - 1)
    def _():
        hre_ref[...] = st_sc[0]
        him_ref[...] = st_sc[1]


def _s5(u_tm, h0_re, h0_im, lw, bn, t):
    tc = t
    while tc * bn > 512 and tc % 2 == 0:
        tc //= 2
    rows = tc * bn
    lc = 256 if bn <= 16 else LANES
    nflat = SSM_FLAT
    consts = [lw['ssm_a_re'], lw['ssm_a_im'], lw['ssm_bbd'], lw['ssm_cbd'], lw['ssm_d'],
              lw['ssm_w_glu'], lw['ssm_b_glu']]
    st_spec = pl.BlockSpec((bn, nflat), lambda c: (0, 0))
    st_sds = jax.ShapeDtypeStruct((bn, nflat), F32)
    return pl.pallas_call(
        functools.partial(_s5_kernel, bn=bn, tc=tc, lc=lc),
        grid=(t // tc,),
        in_specs=[pl.BlockSpec((rows, SSM_WIDTH), lambda c: (c, 0)), st_spec, st_spec]
                 + [_const_spec(a.shape) for a in consts],
        out_specs=(pl.BlockSpec((rows, SSM_WIDTH), lambda c: (c, 0)), st_spec, st_spec),
        out_shape=(jax.ShapeDtypeStruct((t * bn, SSM_WIDTH), BF), st_sds, st_sds),
        scratch_shapes=[pltpu.VMEM((rows, 2 * nflat), F32), pltpu.VMEM((2, bn, nflat), F32)],
        compiler_params=_params("arbitrary"),
        name="s5",
    )(u_tm, h0_re, h0_im, *consts)


def _merge_kernel(x_ref, olat_ref, ob_ref, oc0_ref, oc1_ref, gates_ref, wuv_ref, wa_ref, wb_ref,
                  wc_ref, wout_ref, gffn_ref, *rest, routed):
    if routed:
        router_ref, x1_ref, hf_ref, route_ref = rest
    else:
        x1_ref, hf_ref = rest
    d = x_ref.shape[1]
    hw = SB_GROUP * SB_HEAD_DIM
    o_a = _dot(olat_ref[0], wuv_ref[:KV_RANK, :])
    for h in range(1, MLA_HEADS):
        o_a = o_a + _dot(olat_ref[h], wuv_ref[KV_RANK * h:KV_RANK * (h + 1), :])
    o_a = o_a.astype(BF)
    br_a = _dot(o_a, wa_ref[...])
    br_b = _dot(ob_ref[...], wb_ref[...])
    br_c = _dot(oc0_ref[...], wc_ref[:hw, :]) + _dot(oc1_ref[...], wc_ref[hw:, :])
    merged = (gates_ref[:, :d].astype(F32) * br_a + gates_ref[:, d:2 * d].astype(F32) * br_b
              + gates_ref[:, 2 * d:].astype(F32) * br_c)
    x1 = x_ref[...] + _dot(merged.astype(BF), wout_ref[...])
    x1_ref[...] = x1
    hf = _rms(x1, gffn_ref[...])
    hf_ref[...] = hf.astype(hf_ref.dtype)
    if routed:
        router = router_ref[...]
        r_hi = router.astype(BF)
        r_lo = (router - r_hi.astype(F32)).astype(BF)
        h_hi = hf.astype(BF)
        h_lo = (hf - h_hi.astype(F32)).astype(BF)
        logits = _dot(h_hi, r_hi) + (_dot(h_hi, r_lo) + _dot(h_lo, r_hi))
        lane = lax.broadcasted_iota(jnp.int32, logits.shape, 1)
        lg = jnp.where(lane < N_EXPERTS, logits, NEG_BIG)
        v1 = jnp.max(lg, axis=1, keepdims=True)
        i1 = jnp.min(jnp.where(lg == v1, lane, LANES), axis=1, keepdims=True)
        lg2 = jnp.where(lane == i1, NEG_BIG, lg)
        v2 = jnp.max(lg2, axis=1, keepdims=True)
        i2 = jnp.min(jnp.where(lg2 == v2, lane, LANES), axis=1, keepdims=True)
        e2 = jnp.exp(v2 - v1)
        p1 = 1.0 / (1.0 + e2)
        p2 = e2 / (1.0 + e2)
        route_ref[...] = (jnp.where(lane == 0, i1.astype(F32), 0.0) + jnp.where(lane == 1, i2.astype(F32), 0.0)
                          + jnp.where(lane == 2, p1, 0.0) + jnp.where(lane == 3, p2, 0.0))


def _merge(x, olat, ob, oc0, oc1, gates, lw, router, tm, t):
    n, d = x.shape
    routed = router is not None
    row = lambda w: pl.BlockSpec((tm, w), lambda i: (i, 0))
    consts = [lw['w_uv_bd'], lw['w_br_a'], lw['w_br_b'], lw['w_br_c'], lw['w_out'], lw['g_ffn']]
    if routed:
        consts.append(router)
    acts = [x, olat, ob, oc0, oc1, gates]
    act_specs = [row(a.shape[-1]) for a in acts]
    act_specs[1] = pl.BlockSpec((MLA_HEADS, tm, KV_RANK), lambda i: (0, i, 0))
    if ob.shape[0] != n:
        act_specs[2] = _time_major_spec(tm, t)
    out_shape = [jax.ShapeDtypeStruct((n, d), F32), jax.ShapeDtypeStruct((n, d), F32 if routed else BF)]
    if routed:
        out_shape.append(jax.ShapeDtypeStruct((n, LANES), F32))
    return pl.pallas_call(
        functools.partial(_merge_kernel, routed=routed),
        grid=(n // tm,),
        in_specs=act_specs + [_const_spec(c.shape) for c in consts],
        out_specs=tuple(row(s.shape[1]) for s in out_shape),
        out_shape=tuple(out_shape),
        compiler_params=_params("parallel"),
        name="merge",
    )(*acts, *consts)


def _ffn_kernel(h_ref, x_ref, wg_ref, wu_ref, wd_ref, o_ref, acc_sc):
    f = pl.program_id(1)

    @pl.when(f == 0)
    def _():
        acc_sc[...] = jnp.zeros(acc_sc.shape, F32)

    h = h_ref[...]
    act = (jax.nn.silu(_dot(h, wg_ref[...])) * _dot(h, wu_ref[...])).astype(BF)
    acc_sc[...] += _dot(act, wd_ref[...])

    @pl.when(f == pl.num_programs(1) - 1)
    def _():
        o_ref[...] = x_ref[...] + acc_sc[...]


def _ffn(hf, x, wg, wu, wd, tm, tf):
    n, d = x.shape
    dff = wg.shape[1]
    return pl.pallas_call(
        _ffn_kernel,
        grid=(n // tm, dff // tf),
        in_specs=[pl.BlockSpec((tm, d), lambda i, f: (i, 0)),
                  pl.BlockSpec((tm, d), lambda i, f: (i, 0)),
                  pl.BlockSpec((d, tf), lambda i, f: (0, f)),
                  pl.BlockSpec((d, tf), lambda i, f: (0, f)),
                  pl.BlockSpec((tf, d), lambda i, f: (f, 0))],
        out_specs=pl.BlockSpec((tm, d), lambda i, f: (i, 0)),
        out_shape=jax.ShapeDtypeStruct((n, d), F32),
        scratch_shapes=[pltpu.VMEM((tm, d), F32)],
        compiler_params=_params("parallel", "arbitrary"),
        name="ffn_dense",
    )(hf, x, wg, wu, wd)


def _route_plan(route, tr):
    n = route.shape[0]
    experts = route[:, :2].astype(jnp.int32)
    hot = (experts[:, :, None] == jnp.arange(N_EXPERTS, dtype=jnp.int32)[None, None, :]).astype(jnp.int32).sum(1)
    padded = ((hot.sum(0) + tr - 1) // tr) * tr
    ends = jnp.cumsum(padded)
    dest = (ends - padded)[None, :] + jnp.cumsum(hot, axis=0) - hot
    pos = jnp.take_along_axis(dest, experts, axis=1).astype(jnp.int32)
    n_tiles = (2 * n) // tr + N_EXPERTS
    starts = jnp.arange(n_tiles, dtype=jnp.int32) * tr
    tile_expert = jnp.minimum(jnp.searchsorted(ends, starts, side='right'), N_EXPERTS - 1).astype(jnp.int32)
    n_used = (ends[-1] // tr).astype(jnp.int32).reshape(1)
    return pos[:, 0], pos[:, 1], tile_expert, n_used, n_tiles


def _row_copy_loop(n_rows, make_copies, wait):
    def body(t, carry):
        for cp in make_copies(t):
            if wait:
                cp.wait()
            else:
                cp.start()
        return carry

    lax.fori_loop(0, n_rows, body, 0, unroll=ROW_COPY_UNROLL)


def _dispatch_kernel(pos1_ref, pos2_ref, h_ref, xs_init_ref, xs_ref, sem, *, tm):
    del xs_init_ref
    base = pl.program_id(0) * tm

    def copies(t):
        src = h_ref.at[pl.ds(t, 1), :]
        return [pltpu.make_async_copy(src, xs_ref.at[pl.ds(pos_ref[base + t], 1), :], sem)
                for pos_ref in (pos1_ref, pos2_ref)]

    _row_copy_loop(tm, copies, wait=False)
    _row_copy_loop(tm, copies, wait=True)


def _dispatch(hf, pos1, pos2, n_rows, tm):
    n, d = hf.shape
    grid_spec = pltpu.PrefetchScalarGridSpec(
        num_scalar_prefetch=2,
        grid=(n // tm,),
        in_specs=[pl.BlockSpec((tm, d), lambda i, p1, p2: (i, 0)), pl.BlockSpec(memory_space=pl.ANY)],
        out_specs=pl.BlockSpec(memory_space=pl.ANY),
        scratch_shapes=[pltpu.SemaphoreType.DMA(())],
    )
    return pl.pallas_call(
        functools.partial(_dispatch_kernel, tm=tm),
        grid_spec=grid_spec,
        out_shape=jax.ShapeDtypeStruct((n_rows, d), F32),
        input_output_aliases={3: 0},
        compiler_params=_params("arbitrary"),
        name="moe_dispatch",
    )(pos1, pos2, hf, jnp.zeros((n_rows, d), F32))


def _expert_ffn_kernel(te_ref, nu_ref, xs_ref, wg_ref, wu_ref, wd_ref, ys_ref, acc_sc):
    i = pl.program_id(0)
    f = pl.program_id(1)

    @pl.when(f == 0)
    def _():
        acc_sc[...] = jnp.zeros(acc_sc.shape, F32)

    @pl.when(i < nu_ref[0])
    def _():
        h = xs_ref[...].astype(BF)
        act = (jax.nn.silu(_dot(h, wg_ref[...])) * _dot(h, wu_ref[...])).astype(BF)
        acc_sc[...] += _dot(act, wd_ref[...])

    @pl.when(f == pl.num_programs(1) - 1)
    def _():
        ys_ref[...] = acc_sc[...]


def _expert_ffn(xs, tile_expert, n_used, wg, wu, wd, tr, tf):
    n_rows, d = xs.shape
    dff = wg.shape[2]
    grid_spec = pltpu.PrefetchScalarGridSpec(
        num_scalar_prefetch=2,
        grid=(n_rows // tr, dff // tf),
        in_specs=[pl.BlockSpec((tr, d), lambda i, f, te, nu: (i, 0)),
                  pl.BlockSpec((None, d, tf), lambda i, f, te, nu: (te[i], 0, f)),
                  pl.BlockSpec((None, d, tf), lambda i, f, te, nu: (te[i], 0, f)),
                  pl.BlockSpec((None, tf, d), lambda i, f, te, nu: (te[i], f, 0))],
        out_specs=pl.BlockSpec((tr, d), lambda i, f, te, nu: (i, 0)),
        scratch_shapes=[pltpu.VMEM((tr, d), F32)],
    )
    return pl.pallas_call(
        _expert_ffn_kernel,
        grid_spec=grid_spec,
        out_shape=jax.ShapeDtypeStruct((n_rows, d), F32),
        compiler_params=_params("parallel", "arbitrary"),
        name="moe_experts",
    )(tile_expert, n_used, xs, wg, wu, wd)


def _combine_kernel(pos1_ref, pos2_ref, x_ref, route_ref, ys_ref, o_ref, buf, sems, *, tm):
    i = pl.program_id(0)
    n_steps = pl.num_programs(0)

    def copies_of(step, slot):
        def copies(t):
            return [pltpu.make_async_copy(ys_ref.at[pl.ds(pos_ref[step * tm + t], 1), :],
                                          buf.at[slot, k, pl.ds(t, 1), :], sems.at[slot])
                    for k, pos_ref in enumerate((pos1_ref, pos2_ref))]
        return copies

    @pl.when(i == 0)
    def _():
        _row_copy_loop(tm, copies_of(0, 0), wait=False)

    slot = i % 2
    _row_copy_loop(tm, copies_of(i, slot), wait=True)

    @pl.when(i + 1 < n_steps)
    def _():
        _row_copy_loop(tm, copies_of(i + 1, 1 - slot), wait=False)

    route = route_ref[...]
    o_ref[...] = x_ref[...] + route[:, 2:3] * buf[slot, 0] + route[:, 3:4] * buf[slot, 1]


def _combine(x, route, ys, pos1, pos2, tm):
    n, d = x.shape
    grid_spec = pltpu.PrefetchScalarGridSpec(
        num_scalar_prefetch=2,
        grid=(n // tm,),
        in_specs=[pl.BlockSpec((tm, d), lambda i, p1, p2: (i, 0)),
                  pl.BlockSpec((tm, LANES), lambda i, p1, p2: (i, 0)),
                  pl.BlockSpec(memory_space=pl.ANY)],
        out_specs=pl.BlockSpec((tm, d), lambda i, p1, p2: (i, 0)),
        scratch_shapes=[pltpu.VMEM((2, 2, tm, d), F32), pltpu.SemaphoreType.DMA((2,))],
    )
    return pl.pallas_call(
        functools.partial(_combine_kernel, tm=tm),
        grid_spec=grid_spec,
        out_shape=jax.ShapeDtypeStruct((n, d), F32),
        compiler_params=_params("arbitrary"),
        name="moe_combine",
    )(pos1, pos2, x, route, ys)


def _moe(hf, x, route, wg, wu, wd, tf):
    n, d = x.shape
    tr = MOE_ROW_TILE if 2 * n >= N_EXPERTS * MOE_ROW_TILE else 256
    pos1, pos2, tile_expert, n_used, n_tiles = _route_plan(route, tr)
    tm = _pick_tile(n, 256)
    xs = _dispatch(hf, pos1, pos2, n_tiles * tr, tm)
    ys = _expert_ffn(xs, tile_expert, n_used, wg, wu, wd, tr, tf)
    return _combine(x, route, ys, pos1, pos2, tm)


def _final_norm_kernel(x_ref, g_ref, o_ref):
    o_ref[...] = _rms(x_ref[...], g_ref[...])


def _final_norm(x, g, tm):
    n, d = x.shape
    return pl.pallas_call(
        _final_norm_kernel,
        grid=(n // tm,),
        in_specs=[pl.BlockSpec((tm, d), lambda i: (i, 0)), _const_spec(g.shape)],
        out_specs=pl.BlockSpec((tm, d), lambda i: (i, 0)),
        out_shape=jax.ShapeDtypeStruct((n, d), F32),
        compiler_params=_params("parallel"),
        name="final_norm",
    )(x, g)


def _prep_layer(w, l):
    d = w['w_in'].shape[1]
    lw = {}
    lw['g_mix'] = w['g_mix'][l][None, :]
    w_in = w['w_in'][l]
    split = SEG_KROPE[0] + ROPE_DIM
    lw['w_in_p'] = jnp.concatenate(
        [w_in[:, :split], jnp.zeros((d, ROPE_PAD), w_in.dtype), w_in[:, split:]], axis=1).astype(BF)
    lw['g_q'] = w['g_q'][l][None, :]
    lw['g_kv'] = w['g_kv'][l][None, :]
    half = ROPE_DIM // 2
    w3 = w['w_uq'][l].reshape(Q_RANK, MLA_HEADS, QK_NOPE + ROPE_DIM)
    lw['w_uq_p'] = jnp.concatenate(
        [w3[:, :, :QK_NOPE].reshape(Q_RANK, -1),
         w3[:, :, QK_NOPE:QK_NOPE + half].reshape(Q_RANK, -1),
         w3[:, :, QK_NOPE + half:].reshape(Q_RANK, -1)], axis=1).astype(BF)
    wk = jnp.transpose(w['w_uk'][l], (1, 2, 0))
    zk = jnp.zeros((QK_NOPE, KV_RANK), wk.dtype)
    lw['w_uk_bd'] = jnp.stack([
        jnp.concatenate([jnp.concatenate([wk[2 * p], zk], axis=1),
                         jnp.concatenate([zk, wk[2 * p + 1]], axis=1)], axis=0)
        for p in range(MLA_HEADS // 2)]).astype(BF)
    perm = np.zeros((2 * LANES, MLA_HEADS * LANES), np.float32)
    for h in range(MLA_HEADS):
        for i in range(half):
            perm[h * half + i, LANES * h + i] = 1.0
            perm[LANES + h * half + i, LANES * h + half + i] = 1.0
    lw['perm'] = jnp.asarray(perm, BF)
    eye_h = jnp.eye(MLA_HEADS, dtype=F32)
    lw['w_uv_bd'] = jnp.einsum('rhd,hg->hrgd', w['w_uv'][l], eye_h).reshape(
        MLA_HEADS * KV_RANK, MLA_HEADS * V_DIM).astype(BF)
    for name in ('w_br_a', 'w_br_b', 'w_br_c', 'w_out', 'ssm_w_glu'):
        lw[name] = w[name][l].astype(BF)
    lw['g_ffn'] = w['g_ffn'][l][None, :]

    lam_re, lam_im = w['ssm_lam_re'][l], w['ssm_lam_im'][l]
    dt = jnp.exp(w['ssm_log_dt'][l])[:, None]
    decay = jnp.exp(lam_re * dt)
    ab_re, ab_im = decay * jnp.cos(lam_im * dt), decay * jnp.sin(lam_im * dt)
    inv_den = 1.0 / (lam_re * lam_re + lam_im * lam_im)
    nr, ni = ab_re - 1.0, ab_im
    co_re = (nr * lam_re + ni * lam_im) * inv_den
    co_im = (ni * lam_re - nr * lam_im) * inv_den
    b_re, b_im = w['ssm_b_re'][l], w['ssm_b_im'][l]
    bb_re = co_re[..., None] * b_re - co_im[..., None] * b_im
    bb_im = co_re[..., None] * b_im + co_im[..., None] * b_re
    eye_g = jnp.eye(SSM_GROUPS, dtype=F32)
    bd = lambda bb: jnp.einsum('gnp,gh->gphn', bb, eye_g).reshape(SSM_WIDTH, SSM_FLAT)
    lw['ssm_bbd'] = jnp.concatenate([bd(bb_re), bd(bb_im)], axis=1).astype(BF)
    cd = lambda cc: jnp.einsum('gpn,gh->gnhp', cc, eye_g).reshape(SSM_FLAT, SSM_WIDTH)
    lw['ssm_cbd'] = jnp.concatenate([cd(w['ssm_c_re'][l]), -cd(w['ssm_c_im'][l])], axis=0).astype(BF)
    lw['ssm_a_re'] = ab_re.reshape(1, SSM_FLAT)
    lw['ssm_a_im'] = ab_im.reshape(1, SSM_FLAT)
    lw['ssm_d'] = w['ssm_d'][l][None, :]
    lw['ssm_b_glu'] = w['ssm_b_glu'][l][None, :]
    return lw


def _rope_tables(q_start, t, tm):
    half = ROPE_DIM // 2
    inv = 1.0 / (ROPE_THETA ** (jnp.arange(0, ROPE_DIM, 2, dtype=F32) / ROPE_DIM))
    ang = (q_start + jnp.arange(t)).astype(F32)[:, None] * inv[None, :]
    cos, sin = jnp.cos(ang), jnp.sin(ang)
    z16 = jnp.zeros_like(cos)
    zrest = jnp.zeros((t, LANES - ROPE_DIM), F32)
    tabs = [jnp.tile(cos, (1, MLA_HEADS)), jnp.tile(sin, (1, MLA_HEADS)),
            jnp.concatenate([cos, cos, zrest], axis=1),
            jnp.concatenate([-sin, z16, zrest], axis=1),
            jnp.concatenate([z16, sin, zrest], axis=1)]
    if tm > t:
        tabs = [jnp.tile(a, (tm // t, 1)) for a in tabs]
    return tabs


def _trunk(x, q_start, caches, w, layers, ffn_w):
    bsz, t, d = x.shape
    n = bsz * t
    xf = x.reshape(n, d)
    tm = _pick_tile(n, 256)
    assert t % tm == 0 or tm % t == 0
    tabs = _rope_tables(q_start, t, tm)
    new_rows = []
    for l, lw in enumerate(layers):
        (qcat, kcat, lat, krope, u, qsb0, qsb1, ksb, vsb, gates) = _in_proj(xf, lw, tabs, tm, bsz, t)
        time_major = u.shape[0] != n
        if time_major:
            u_tm = u.reshape(n, SSM_WIDTH)
        else:
            u_tm = u.reshape(bsz, t, SSM_WIDTH).transpose(1, 0, 2).reshape(n, SSM_WIDTH)
        if caches is None:
            olat = _mla_prompt(qcat, kcat, bsz, t)
            oc0, oc1 = _sb_prompt(qsb0, qsb1, ksb, vsb, bsz, t)
            h0_re = jnp.zeros((bsz, SSM_FLAT), F32)
            h0_im = h0_re
        else:
            lat_pool, kr_pool, k_pool, v_pool, s_re, s_im, page_table = caches
            olat = _mla_sample(qcat, kcat, lat_pool, kr_pool, page_table, l, bsz, t)
            oc0, oc1 = _sb_sample(qsb0, qsb1, ksb, vsb, k_pool, v_pool, page_table, l, bsz, t)
            h0_re = s_re[l].reshape(bsz, SSM_FLAT)
            h0_im = s_im[l].reshape(bsz, SSM_FLAT)
        ob_tm, h_re, h_im = _s5(u_tm, h0_re, h0_im, lw, bsz, t)
        if time_major:
            ob = ob_tm.reshape(t, bsz * SSM_WIDTH)
        else:
            ob = ob_tm.reshape(t, bsz, SSM_WIDTH).transpose(1, 0, 2).reshape(n, SSM_WIDTH)
        i = l // 2
        tm_f = _pick_tile(n, 1024)
        if l % 2 == 0:
            x1, hf = _merge(xf, olat, ob, oc0, oc1, gates, lw, None, tm, t)
            wg, wu, wd = ffn_w['dense'][i]
            xf = _ffn(hf, x1, wg, wu, wd, tm_f, 256)
        else:
            x1, hf, gate = _merge(xf, olat, ob, oc0, oc1, gates, lw, ffn_w['router'][i], tm, t)
            wg, wu, wd = ffn_w['moe'][i]
            xf = _moe(hf, x1, gate, wg, wu, wd, 512)
        new_rows.append((lat.reshape(bsz, t, KV_RANK), krope.reshape(bsz, t, ROPE_DIM),
                         ksb.reshape(bsz, t, SB_KV_HEADS, SB_HEAD_DIM),
                         vsb.reshape(bsz, t, SB_KV_HEADS, SB_HEAD_DIM),
                         h_re.reshape(bsz, SSM_GROUPS, SSM_STATE),
                         h_im.reshape(bsz, SSM_GROUPS, SSM_STATE)))
    y = _final_norm(xf, w['g_final'][None, :], tm).reshape(bsz, t, d)
    stacked = [jnp.stack([r[j] for r in new_rows]) for j in range(6)]
    return y, stacked


def kernel(x_prompt, x_sample, cache_mla_latent, cache_mla_krope, cache_sb_k, cache_sb_v, state_ssm_re, state_ssm_im, page_table, g_mix, w_in, g_q, g_kv, w_uq, w_uk, w_uv, ssm_lam_re, ssm_lam_im, ssm_log_dt, ssm_b_re, ssm_b_im, ssm_c_re, ssm_c_im, ssm_d, ssm_w_glu, ssm_b_glu, w_br_a, w_br_b, w_br_c, w_out, g_ffn, ffn_w_gate, ffn_w_up, ffn_w_down, moe_router, moe_w_gate, moe_w_up, moe_w_down, g_final):
    w = dict(g_mix=g_mix, w_in=w_in, g_q=g_q, g_kv=g_kv, w_uq=w_uq, w_uk=w_uk, w_uv=w_uv,
             ssm_lam_re=ssm_lam_re, ssm_lam_im=ssm_lam_im, ssm_log_dt=ssm_log_dt,
             ssm_b_re=ssm_b_re, ssm_b_im=ssm_b_im, ssm_c_re=ssm_c_re, ssm_c_im=ssm_c_im,
             ssm_d=ssm_d, ssm_w_glu=ssm_w_glu, ssm_b_glu=ssm_b_glu, w_br_a=w_br_a,
             w_br_b=w_br_b, w_br_c=w_br_c, w_out=w_out, g_ffn=g_ffn, g_final=g_final)
    depth = w_in.shape[0]
    layers = [_prep_layer(w, l) for l in range(depth)]
    d = w_in.shape[1]
    router = jnp.pad(moe_router, ((0, 0), (0, 0), (0, LANES - N_EXPERTS)))
    ffn_w = {
        'dense': [(ffn_w_gate[i].astype(BF), ffn_w_up[i].astype(BF), ffn_w_down[i].astype(BF))
                  for i in range(ffn_w_gate.shape[0])],
        'moe': [(moe_w_gate[i].astype(BF), moe_w_up[i].astype(BF), moe_w_down[i].astype(BF))
                for i in range(moe_w_gate.shape[0])],
        'router': [router[i] for i in range(router.shape[0])],
    }
    past_len = page_table.shape[1] * PAGE_KEYS
    y_p, (lat_p, kr_p, k_p, v_p, re_p, im_p) = _trunk(x_prompt, 0, None, w, layers, ffn_w)
    caches = (cache_mla_latent, cache_mla_krope, cache_sb_k, cache_sb_v,
              state_ssm_re, state_ssm_im, page_table)
    y_s, (lat_s, kr_s, k_s, v_s, re_s, im_s) = _trunk(x_sample, past_len, caches, w, layers, ffn_w)
    return (y_p, y_s, lat_p, lat_s, kr_p, kr_s, k_p, k_s, v_p, v_s, re_p, re_s, im_p, im_s)
```
